```python
import math
import jax
import jax.numpy as jnp
from jax import lax
import numpy as np

D_MODEL = 1024
BATCH = 16
SEQ = 2048
DEPTH = 4

N_MIXERS = 4
NORM_EPS = 1e-6
ROPE_THETA = 10000.0
HEAD_DIM = 64
NEG_INF = -1e30

POOL_WINDOWS = (2, 4, 8, 16)
POOL_GROUP = D_MODEL // len(POOL_WINDOWS)

DIFF_HEADS = D_MODEL // (2 * HEAD_DIM)
DIFF_SUBLN_EPS = 1e-5
Q_BLOCK = 128

LRU_WIDTH = D_MODEL
LRU_BLOCKS = D_MODEL // HEAD_DIM
LRU_BLOCK_DIM = LRU_WIDTH // LRU_BLOCKS
LRU_CONV = 4
LRU_C = 8.0

DIL_GROUPS = ((128, 1), (512, 4), (2048, 16))
DIL_HEADS = D_MODEL // HEAD_DIM
DIL_WIDTH = DIL_HEADS * HEAD_DIM

D_FF = 2816
FFN_CONV = 3

kernel_name = "hybrid_pool_diffattn_rglru_dilated_encoder"


def _n_layers_of(mixer):
    return len(range(mixer, DEPTH, N_MIXERS))


def _rms_norm(x, gain, eps=NORM_EPS):
    xf = x.astype(jnp.float32)
    y = xf * lax.rsqrt(jnp.mean(xf * xf, axis=-1, keepdims=True) + eps)
    return (y * gain.astype(jnp.float32)).astype(x.dtype)


def _rope_tables(positions, head_dim):
    inv = ROPE_THETA ** (-jnp.arange(0, head_dim, 2, dtype=jnp.float32) / head_dim)
    ang = positions.astype(jnp.float32)[:, None] * inv[None, :]
    return jnp.cos(ang), jnp.sin(ang)


def _rope(x, cos, sin):
    shape = (cos.shape[0],) + (1,) * (x.ndim - 3) + (cos.shape[1],)
    c, s = cos.reshape(shape), sin.reshape(shape)
    xf = x.astype(jnp.float32)
    x1, x2 = jnp.split(xf, 2, axis=-1)
    return jnp.concatenate([x1 * c - x2 * s, x2 * c + x1 * s], axis=-1).astype(x.dtype)


def _pool_mixer(x, w, scale):
    b, s, d = x.shape
    xf = x.astype(jnp.float32)
    cs = jnp.concatenate([jnp.zeros((b, 1, d), jnp.float32), lax.cumsum(xf, axis=1)], axis=1)
    t = jnp.arange(s)
    parts = []
    for g, win in enumerate(POOL_WINDOWS):
        lo = jnp.clip(t - win // 2, 0, s)
        hi = jnp.clip(t + win - win // 2, 0, s)
        sl = slice(g * POOL_GROUP, (g + 1) * POOL_GROUP)
        csg = cs[:, :, sl]
        mean = (csg[:, hi] - csg[:, lo]) / (hi - lo).astype(jnp.float32)[:, None]
        parts.append(mean - xf[:, :, sl])
    pooled = jnp.stack(parts, axis=2).astype(x.dtype)
    y = jnp.einsum('bsgc,gce->bsge', pooled, w).reshape(b, s, d)
    return y * scale


def _diff_attention(x, w_qkv, lq1, lk1, lq2, lk2, subln, w_o, cos, sin, layer_idx):
    b, s, _ = x.shape
    h, dh = DIFF_HEADS, HEAD_DIM
    q, k, v = jnp.split(x @ w_qkv, 3, axis=-1)
    q = _rope(q.reshape(b, s, 2 * h, dh), cos, sin).reshape(b, s, h, 2, dh) * (dh ** -0.5)
    k = _rope(k.reshape(b, s, 2 * h, dh), cos, sin).reshape(b, s, h, 2, dh)
    v = v.reshape(b, s, h, 2 * dh)
    lam_init = 0.8 - 0.6 * math.exp(-0.3 * layer_idx)
    lam = (jnp.exp(jnp.sum(lq1.astype(jnp.float32) * lk1.astype(jnp.float32)))
           - jnp.exp(jnp.sum(lq2.astype(jnp.float32) * lk2.astype(jnp.float32))) + lam_init)
    nb = s // Q_BLOCK
    q_blocks = q.reshape(b, nb, Q_BLOCK, h, 2, dh).swapaxes(0, 1)

    def one_block(qb):
        sc = jnp.einsum('bqhce,bkhce->bhcqk', qb, k).astype(jnp.float32)
        p = jax.nn.softmax(sc, axis=-1)
        a = p[:, :, 0] - lam * p[:, :, 1]
        return jnp.einsum('bhqk,bkhe->bqhe', a.astype(v.dtype), v)

    o = lax.map(one_block, q_blocks).swapaxes(0, 1).reshape(b, s, h, 2 * dh)
    o = _rms_norm(o, subln, DIFF_SUBLN_EPS) * (1.0 - lam_init)
    return o.reshape(b, s, h * 2 * dh) @ w_o


def _causal_conv(x, w, bias):
    kw = w.shape[0]
    s = x.shape[1]
    xp = jnp.pad(x, ((0, 0), (kw - 1, 0), (0, 0)))
    y = bias
    for j in range(kw):
        y = y + w[j] * xp[:, j:j + s]
    return y


def _rg_lru_direction(u, conv_w, conv_b, w_a, b_a, w_x, b_x, lam):
    b, s, c = u.shape
    xc = _causal_conv(u, conv_w, conv_b)
    xb = xc.reshape(b, s, LRU_BLOCKS, LRU_BLOCK_DIM)
    r = jax.nn.sigmoid((jnp.einsum('bsni,nij->bsnj', xb, w_a).reshape(b, s, c) + b_a).astype(jnp.float32))
    i = jax.nn.sigmoid((jnp.einsum('bsni,nij->bsnj', xb, w_x).reshape(b, s, c) + b_x).astype(jnp.float32))
    log_a = LRU_C * r * jax.nn.log_sigmoid(lam.astype(jnp.float32))
    a = jnp.exp(log_a)
    inp = jnp.sqrt(-jnp.expm1(2.0 * log_a)) * (i * xc.astype(jnp.float32))

    def combine(left, right):
        a_l, b_l = left
        a_r, b_r = right
        return a_l * a_r, a_r * b_l + b_r

    _, h = lax.associative_scan(combine, (a, inp), axis=1)
    return h


def _rglru_mixer(x, w_in, conv_w, conv_b, w_a, b_a, w_x, b_x, lam, w_out):
    gate, u = jnp.split(x @ w_in, 2, axis=-1)
    h_fwd = _rg_lru_direction(u, conv_w[0], conv_b[0], w_a[0], b_a[0], w_x[0], b_x[0], lam[0])
    h_bwd = jnp.flip(_rg_lru_direction(jnp.flip(u, axis=1), conv_w[1], conv_b[1], w_a[1], b_a[1],
                                       w_x[1], b_x[1], lam[1]), axis=1)
    y = jax.nn.gelu(gate.astype(jnp.float32), approximate=True) * (h_fwd + h_bwd)
    return y.astype(x.dtype) @ w_out


def _dilated_chain_attention(q, k, v, dilation, half):
    b, s, h, dh = q.shape
    n = s // dilation
    blk = half
    nb = -(-n // blk)
    pad = nb * blk - n

    def chains(t):
        return t.reshape(b, n, dilation, h, dh).transpose(0, 2, 3, 1, 4)

    qc = jnp.pad(chains(q), ((0, 0), (0, 0), (0, 0), (0, pad), (0, 0))).reshape(b, dilation, h, nb, blk, dh)

    def windows(t):
        tp = jnp.pad(chains(t), ((0, 0), (0, 0), (0, 0), (blk, blk + pad), (0, 0)))
        tp = tp.reshape(b, dilation, h, nb + 2, blk, dh)
        return jnp.concatenate([tp[:, :, :, :-2], tp[:, :, :, 1:-1], tp[:, :, :, 2:]], axis=4)

    kw, vw = windows(k), windows(v)
    qpos = jnp.arange(nb * blk).reshape(nb, blk)
    kpos = jnp.arange(nb)[:, None] * blk + jnp.arange(-blk, 2 * blk)[None, :]
    rel = kpos[:, None, :] - qpos[:, :, None]
    valid = (jnp.abs(rel) <= half) & (kpos[:, None, :] >= 0) & (kpos[:, None, :] < n)
    sc = jnp.einsum('bdhnqe,bdhnke->bdhnqk', qc, kw).astype(jnp.float32)
    sc = jnp.where(valid, sc, NEG_INF)
    lse = jax.nn.logsumexp(sc, axis=-1)
    p = jnp.exp(sc - lse[..., None])
    o = jnp.einsum('bdhnqk,bdhnke->bdhnqe', p.astype(v.dtype), vw)

    def unchain(t):
        t = t.reshape((b, dilation, h, nb * blk) + t.shape[5:])[:, :, :, :n]
        t = jnp.moveaxis(t, 3, 1)
        return t.reshape((b, s, h) + t.shape[4:])

    return unchain(o), unchain(lse)


def _dilated_mixer(x, w_qkv, w_o, cos, sin):
    b, s, _ = x.shape
    n_groups = len(DIL_GROUPS)
    qkv = (x @ w_qkv).reshape(b, s, n_groups, 3, DIL_HEADS, HEAD_DIM)
    q = _rope(qkv[:, :, :, 0], cos, sin) * (HEAD_DIM ** -0.5)
    k = _rope(qkv[:, :, :, 1], cos, sin)
    v = qkv[:, :, :, 2]
    outs, lses = [], []
    for g, (window, dilation) in enumerate(DIL_GROUPS):
        o, l = _dilated_chain_attention(q[:, :, g], k[:, :, g], v[:, :, g], dilation, window // (2 * dilation))
        outs.append(o.astype(jnp.float32))
        lses.append(l)
    wts = jax.nn.softmax(jnp.stack(lses, axis=0), axis=0)
    o = jnp.sum(wts[..., None] * jnp.stack(outs, axis=0), axis=0)
    return o.astype(x.dtype).reshape(b, s, DIL_WIDTH) @ w_o


def _conv_ffn(x, w_up, conv_w, conv_b, w_down):
    s = x.shape[1]
    g, u = jnp.split(x @ w_up, 2, axis=-1)
    half = FFN_CONV // 2
    gp = jnp.pad(g, ((0, 0), (half, half), (0, 0)))
    gc = conv_b
    for j in range(FFN_CONV):
        gc = gc + conv_w[j] * gp[:, j:j + s]
    act = jax.nn.gelu(gc.astype(jnp.float32), approximate=False).astype(x.dtype)
    return (act * u) @ w_down


def setup_inputs(seed: int = 0) -> dict:
    key = jax.random.key(seed)
    keys = iter(jax.random.split(key, 48))
    f32 = jnp.float32
    n_a, n_b, n_c, n_d = (_n_layers_of(m) for m in range(N_MIXERS))

    def dense(shape, fan_in):
        return jax.random.normal(next(keys), shape, f32) * fan_in ** -0.5

    def gain(shape):
        return 1.0 + 0.05 * jax.random.normal(next(keys), shape, f32)

    def small(shape, scale=0.02):
        return scale * jax.random.normal(next(keys), shape, f32)

    x = jax.random.normal(next(keys), (BATCH, SEQ, D_MODEL), f32)
    a_c = jax.random.uniform(next(keys), (n_c, 2, LRU_WIDTH), f32, 0.9, 0.999)
    base = a_c ** (1.0 / LRU_C)
    lru_lambda = jnp.log(base) - jnp.log1p(-base)
    return {
        "x": x,
        "positions": jnp.arange(SEQ, dtype=jnp.int32),
        "mix_norm": gain((DEPTH, D_MODEL)),
        "pool_w": dense((n_a, len(POOL_WINDOWS), POOL_GROUP, POOL_GROUP), POOL_GROUP),
        "pool_scale": gain((n_a, D_MODEL)),
        "diff_w_qkv": dense((n_b, D_MODEL, 3 * D_MODEL), D_MODEL),
        "diff_lam_q1": small((n_b, HEAD_DIM), 0.1),
        "diff_lam_k1": small((n_b, HEAD_DIM), 0.1),
        "diff_lam_q2": small((n_b, HEAD_DIM), 0.1),
        "diff_lam_k2": small((n_b, HEAD_DIM), 0.1),
        "diff_subln": gain((n_b, 2 * HEAD_DIM)),
        "diff_w_o": dense((n_b, D_MODEL, D_MODEL), D_MODEL),
        "lru_w_in": dense((n_c, D_MODEL, 2 * LRU_WIDTH), D_MODEL),
        "lru_conv_w": dense((n_c, 2, LRU_CONV, LRU_WIDTH), LRU_CONV),
        "lru_conv_b": small((n_c, 2, LRU_WIDTH)),
        "lru_w_a": dense((n_c, 2, LRU_BLOCKS, LRU_BLOCK_DIM, LRU_BLOCK_DIM), LRU_BLOCK_DIM),
        "lru_b_a": small((n_c, 2, LRU_WIDTH)),
        "lru_w_x": dense((n_c, 2, LRU_BLOCKS, LRU_BLOCK_DIM, LRU_BLOCK_DIM), LRU_BLOCK_DIM),
        "lru_b_x": small((n_c, 2, LRU_WIDTH)),
        "lru_lambda": lru_lambda,
        "lru_w_out": dense((n_c, LRU_WIDTH, D_MODEL), LRU_WIDTH),
        "dil_w_qkv": dense((n_d, D_MODEL, len(DIL_GROUPS) * 3 * DIL_WIDTH), D_MODEL),
        "dil_w_o": dense((n_d, DIL_WIDTH, D_MODEL), DIL_WIDTH),
        "ffn_norm": gain((DEPTH, D_MODEL)),
        "ffn_w_up": dense((DEPTH, D_MODEL, 2 * D_FF), D_MODEL),
        "ffn_conv_w": dense((DEPTH, FFN_CONV, D_FF), FFN_CONV),
        "ffn_conv_b": small((DEPTH, D_FF)),
        "ffn_w_down": dense((DEPTH, D_FF, D_MODEL), D_FF),
        "final_norm": gain((D_MODEL,)),
    }


def reference(x, positions, mix_norm, pool_w, pool_scale, diff_w_qkv, diff_lam_q1, diff_lam_k1,
              diff_lam_q2, diff_lam_k2, diff_subln, diff_w_o, lru_w_in, lru_conv_w, lru_conv_b,
              lru_w_a, lru_b_a, lru_w_x, lru_b_x, lru_lambda, lru_w_out, dil_w_qkv, dil_w_o,
              ffn_norm, ffn_w_up, ffn_conv_w, ffn_conv_b, ffn_w_down, final_norm):
    cos, sin = _rope_tables(positions, HEAD_DIM)
    h = x
    for i in range(DEPTH):
        m, j = i % N_MIXERS, i // N_MIXERS
        hn = _rms_norm(h, mix_norm[i])
        if m == 0:
            y = _pool_mixer(hn, pool_w[j], pool_scale[j])
        elif m == 1:
            y = _diff_attention(hn, diff_w_qkv[j], diff_lam_q1[j], diff_lam_k1[j], diff_lam_q2[j],
                                diff_lam_k2[j], diff_subln[j], diff_w_o[j], cos, sin, i)
        elif m == 2:
            y = _rglru_mixer(hn, lru_w_in[j], lru_conv_w[j], lru_conv_b[j], lru_w_a[j], lru_b_a[j],
                             lru_w_x[j], lru_b_x[j], lru_lambda[j], lru_w_out[j])
        else:
            y = _dilated_mixer(hn, dil_w_qkv[j], dil_w_o[j], cos, sin)
        h = h + y
        h = h + _conv_ffn(_rms_norm(h, ffn_norm[i]), ffn_w_up[i], ffn_conv_w[i], ffn_conv_b[i], ffn_w_down[i])
    return _rms_norm(h, final_norm)
```

```python
import functools
import math

import jax
import jax.numpy as jnp
from jax import lax
from jax.experimental import pallas as pl
from jax.experimental.pallas import tpu as pltpu

F32 = jnp.float32
BF16 = jnp.bfloat16

HEAD_DIM = 64
NORM_EPS = 1e-6
ROPE_THETA = 10000.0
NEG_INF = -1e30
POOL_WINDOWS = (2, 4, 8, 16)
DIFF_SUBLN_EPS = 1e-5
LRU_C = 8.0
LRU_CONV = 4
DIL_GROUPS = ((128, 1), (512, 4), (2048, 16))
FFN_CONV = 3
N_MIXERS = 4

LANES = 128
SUBLANES = 8
VMEM_LIMIT_BYTES = 56 * 1024 * 1024


def _params(*semantics):
    return pltpu.CompilerParams(dimension_semantics=semantics, vmem_limit_bytes=VMEM_LIMIT_BYTES)


def _rms_rows(x, gain, eps):
    ms = jnp.mean(x * x, axis=-1, keepdims=True)
    return x * lax.rsqrt(ms + eps) * gain


def _halo_window(ref, r0, rows, cols=slice(None)):
    return ref[pl.ds(r0 - SUBLANES, rows + 2 * SUBLANES), cols]


def _shift_rows(win, off):
    if off == 0:
        return win
    return pltpu.roll(win, (-off) % win.shape[0], 0)


def _rope_table_kernel(pos_ref, inv_ref, cos_ref, sin_ref):
    ang = pos_ref[...] * inv_ref[...]
    lane = lax.broadcasted_iota(jnp.int32, ang.shape, 1)
    first_half = (lane % HEAD_DIM) < (HEAD_DIM // 2)
    cos_ref[...] = jnp.cos(ang)
    sin_ref[...] = jnp.where(first_half, -jnp.sin(ang), jnp.sin(ang))


def _rope_tables(positions):
    s = positions.shape[0]
    inv = ROPE_THETA ** (-jnp.arange(0, HEAD_DIM, 2, dtype=F32) / HEAD_DIM)
    inv = jnp.tile(inv, LANES // (HEAD_DIM // 2)).reshape(1, LANES)
    pos = positions.astype(F32).reshape(s, 1)
    return pl.pallas_call(
        _rope_table_kernel,
        out_shape=(jax.ShapeDtypeStruct((s, LANES), F32), jax.ShapeDtypeStruct((s, LANES), F32)),
        name="rope_tables",
    )(pos, inv)


def _qkv_proj_kernel(x_ref, g_ref, w_ref, cos_ref, sin_ref, o_ref, xn_ref, *, dsub, n_tile, d_model):
    j = pl.program_id(3)
    n_chunks = w_ref.shape[1] // LANES

    @pl.when(j == 0)
    def _():
        for r in range(dsub):
            x = x_ref[0, :, r * d_model:(r + 1) * d_model]
            xn_ref[r * n_tile:(r + 1) * n_tile, :] = _rms_rows(x, g_ref[...], NORM_EPS).astype(BF16)

    y = jnp.dot(xn_ref[...], w_ref[...], preferred_element_type=F32)
    kind = j % 3

    def store(c, val):
        for r in range(dsub):
            o_ref[0, c, r] = val[r * n_tile:(r + 1) * n_tile, :].astype(BF16)

    @pl.when(kind == 2)
    def _():
        for c in range(n_chunks):
            store(c, y[:, c * LANES:(c + 1) * LANES])

    def rope_store(scale):
        if dsub == 1:
            ct, st = cos_ref[...], sin_ref[...]
        else:
            ct = jnp.concatenate([cos_ref[:, r * LANES:(r + 1) * LANES] for r in range(dsub)], axis=0)
            st = jnp.concatenate([sin_ref[:, r * LANES:(r + 1) * LANES] for r in range(dsub)], axis=0)
        lane = lax.broadcasted_iota(jnp.int32, (1, LANES), 1)
        first_half = (lane % HEAD_DIM) < (HEAD_DIM // 2)
        for c in range(n_chunks):
            yc = y[:, c * LANES:(c + 1) * LANES]
            partner = jnp.where(first_half, pltpu.roll(yc, LANES - HEAD_DIM // 2, 1),
                                pltpu.roll(yc, HEAD_DIM // 2, 1))
            out = yc * ct + partner * st
            if scale != 1.0:
                out = out * scale
            store(c, out)

    @pl.when(kind == 0)
    def _():
        rope_store(HEAD_DIM ** -0.5)

    @pl.when(kind == 1)
    def _():
        rope_store(1.0)


def _qkv_proj(h, gain, w, col0, n_cols, cos, sin, dilation):
    b, s, dm = h.shape
    d = dilation
    n = s // d
    tm = 1024
    tn = 1024
    dsub = min(d, 8)
    n_tile = tm // dsub
    if n_tile > n:
        n_tile, dsub = n, tm // n
    assert n % n_tile == 0 and d % dsub == 0 and n_cols % tn == 0 and col0 % tn == 0
    xv = h.reshape(b, n, d * dm)
    cv = cos.reshape(n, d * LANES)
    sv = sin.reshape(n, d * LANES)
    jt0 = col0 // tn
    grid = (b, d // dsub, n // n_tile, n_cols // tn)
    return pl.pallas_call(
        functools.partial(_qkv_proj_kernel, dsub=dsub, n_tile=n_tile, d_model=dm),
        grid=grid,
        in_specs=[
            pl.BlockSpec((1, n_tile, dsub * dm), lambda bi, rb, i, j: (bi, i, rb)),
            pl.BlockSpec((1, dm), lambda bi, rb, i, j: (0, 0)),
            pl.BlockSpec((dm, tn), lambda bi, rb, i, j: (0, jt0 + j)),
            pl.BlockSpec((n_tile, dsub * LANES), lambda bi, rb, i, j: (i, rb)),
            pl.BlockSpec((n_tile, dsub * LANES), lambda bi, rb, i, j: (i, rb)),
        ],
        out_specs=pl.BlockSpec((1, tn // LANES, dsub, n_tile, LANES), lambda bi, rb, i, j: (bi, j, rb, i, 0)),
        out_shape=jax.ShapeDtypeStruct((b, n_cols // LANES, d, n, LANES), BF16),
        scratch_shapes=[pltpu.VMEM((tm, dm), BF16)],
        compiler_params=_params("parallel", "parallel", "parallel", "arbitrary"),
        name=f"qkv_proj_d{d}",
    )(xv, gain.reshape(1, dm), w, cv, sv)


def _norm_proj_kernel(x_ref, g_ref, w_ref, o_ref, xn_ref):
    @pl.when(pl.program_id(1) == 0)
    def _():
        xn_ref[...] = _rms_rows(x_ref[...], g_ref[...], NORM_EPS).astype(BF16)

    o_ref[...] = jnp.dot(xn_ref[...], w_ref[...], preferred_element_type=F32)


def _norm_proj(h2d, gain, w):
    m, dm = h2d.shape
    n = w.shape[1]
    tm, tn = 1024, 1024
    assert m % tm == 0 and n % tn == 0
    return pl.pallas_call(
        _norm_proj_kernel,
        grid=(m // tm, n // tn),
        in_specs=[
            pl.BlockSpec((tm, dm), lambda i, j: (i, 0)),
            pl.BlockSpec((1, dm), lambda i, j: (0, 0)),
            pl.BlockSpec((dm, tn), lambda i, j: (0, j)),
        ],
        out_specs=pl.BlockSpec((tm, tn), lambda i, j: (i, j)),
        out_shape=jax.ShapeDtypeStruct((m, n), F32),
        scratch_shapes=[pltpu.VMEM((tm, dm), BF16)],
        compiler_params=_params("parallel", "arbitrary"),
        name="norm_proj",
    )(h2d, gain.reshape(1, dm), w)


def _out_proj_hm_kernel(a_ref, w_ref, h_ref, o_ref):
    a = jnp.concatenate([a_ref[0, p] for p in range(a_ref.shape[1])], axis=-1)
    o_ref[0] = h_ref[0] + jnp.dot(a, w_ref[...], preferred_element_type=F32)


def _out_proj_hm(a_hm, w, h):
    b, s, dm = h.shape
    kb = a_hm.shape[1]
    tm = 1024
    return pl.pallas_call(
        _out_proj_hm_kernel,
        grid=(b, s // tm),
        in_specs=[
            pl.BlockSpec((1, kb, tm, LANES), lambda bi, i: (bi, 0, i, 0)),
            pl.BlockSpec(w.shape, lambda bi, i: (0, 0)),
            pl.BlockSpec((1, tm, dm), lambda bi, i: (bi, i, 0)),
        ],
        out_specs=pl.BlockSpec((1, tm, dm), lambda bi, i: (bi, i, 0)),
        out_shape=jax.ShapeDtypeStruct(h.shape, F32),
        compiler_params=_params("parallel", "parallel"),
        name="out_proj_hm",
    )(a_hm, w, h)


def _out_proj_kernel(a_ref, w_ref, h_ref, o_ref):
    o_ref[0] = h_ref[0] + jnp.dot(a_ref[0], w_ref[...], preferred_element_type=F32)


def _out_proj(a, w, h):
    b, s, dm = h.shape
    k = a.shape[-1]
    tm = 1024
    return pl.pallas_call(
        _out_proj_kernel,
        grid=(b, s // tm),
        in_specs=[
            pl.BlockSpec((1, tm, k), lambda bi, i: (bi, i, 0)),
            pl.BlockSpec(w.shape, lambda bi, i: (0, 0)),
            pl.BlockSpec((1, tm, dm), lambda bi, i: (bi, i, 0)),
        ],
        out_specs=pl.BlockSpec((1, tm, dm), lambda bi, i: (bi, i, 0)),
        out_shape=jax.ShapeDtypeStruct(h.shape, F32),
        compiler_params=_params("parallel", "parallel"),
        name="out_proj",
    )(a, w, h)


POOL_PAD = 8
POOL_ROWS = 256


def _pool_kernel(h_ref, g_ref, w_ref, sc_ref, o_ref, hn_ref):
    s, dm = h_ref.shape[1], h_ref.shape[2]
    group = dm // len(POOL_WINDOWS)
    zeros = jnp.zeros((POOL_PAD, dm), F32)
    hn_ref[0:POOL_PAD, :] = zeros
    hn_ref[POOL_PAD + s:POOL_PAD + s + POOL_PAD, :] = zeros

    def norm_step(i, carry):
        r0 = pl.multiple_of(i * POOL_ROWS, POOL_ROWS)
        hn_ref[pl.ds(POOL_PAD + r0, POOL_ROWS), :] = _rms_rows(h_ref[0, pl.ds(r0, POOL_ROWS), :], g_ref[...], NORM_EPS)
        return carry

    lax.fori_loop(0, s // POOL_ROWS, norm_step, 0)

    def mix_step(i, carry):
        r0 = pl.multiple_of(i * POOL_ROWS, POOL_ROWS)
        t = r0 + lax.broadcasted_iota(jnp.int32, (POOL_ROWS, 1), 0)
        inner = slice(SUBLANES, SUBLANES + POOL_ROWS)
        for g, win in enumerate(POOL_WINDOWS):
            half = win // 2
            sl = slice(g * group, (g + 1) * group)
            xw = _halo_window(hn_ref, POOL_PAD + r0, POOL_ROWS, sl)
            tot = xw + _shift_rows(xw, -1)
            step = 1
            while 2 * step < win:
                tot = _shift_rows(tot, -step) + _shift_rows(tot, step)
                step *= 2
            cnt = (jnp.minimum(t + half, s) - jnp.maximum(t - half, 0)).astype(F32)
            pooled = (tot[inner] / cnt - xw[inner]).astype(BF16)
            y = jnp.dot(pooled, w_ref[g], preferred_element_type=F32) * sc_ref[:, sl]
            o_ref[0, pl.ds(r0, POOL_ROWS), sl] = h_ref[0, pl.ds(r0, POOL_ROWS), sl] + y
        return carry

    lax.fori_loop(0, s // POOL_ROWS, mix_step, 0)


def _pool_mixer(h, gain, w, scale):
    b, s, dm = h.shape
    return pl.pallas_call(
        _pool_kernel,
        grid=(b,),
        in_specs=[
            pl.BlockSpec((1, s, dm), lambda bi: (bi, 0, 0)),
            pl.BlockSpec((1, dm), lambda bi: (0, 0)),
            pl.BlockSpec(w.shape, lambda bi: (0, 0, 0)),
            pl.BlockSpec((1, dm), lambda bi: (0, 0)),
        ],
        out_specs=pl.BlockSpec((1, s, dm), lambda bi: (bi, 0, 0)),
        out_shape=jax.ShapeDtypeStruct(h.shape, F32),
        scratch_shapes=[pltpu.VMEM((s + 2 * POOL_PAD, dm), F32)],
        compiler_params=_params("parallel"),
        name="pool_mixer",
    )(h, gain.reshape(1, dm), w.astype(BF16), scale.reshape(1, dm))


def _diff_attn_kernel(q_ref, k_ref, v_ref, lam_ref, sub_ref, o_ref, *, lam_init):
    q = q_ref[0, 0, 0]
    k = k_ref[0, 0, 0]
    v = v_ref[0, 0, 0]
    lp = lam_ref[...]
    lam = (jnp.exp(jnp.sum(lp[0:1] * lp[1:2], axis=-1, keepdims=True))
           - jnp.exp(jnp.sum(lp[2:3] * lp[3:4], axis=-1, keepdims=True)) + lam_init)
    lane = lax.broadcasted_iota(jnp.int32, (1, LANES), 1)
    first = lane < HEAD_DIM
    zero = jnp.zeros_like(q)
    dn = (((1,), (1,)), ((), ()))

    def softmax_parts(qm):
        sc = lax.dot_general(qm, k, dn, preferred_element_type=F32)
        p = jnp.exp(sc - jnp.max(sc, axis=-1, keepdims=True))
        return p, jnp.sum(p, axis=-1, keepdims=True)

    p0, l0 = softmax_parts(jnp.where(first, q, zero))
    p1, l1 = softmax_parts(jnp.where(first, zero, q))
    a = p0 * (1.0 / l0) - p1 * (lam / l1)
    o = jnp.dot(a.astype(BF16), v, preferred_element_type=F32)
    o = _rms_rows(o, sub_ref[...], DIFF_SUBLN_EPS) * (1.0 - lam_init)
    o_ref[0, 0] = o.astype(BF16)


def _diff_attention(qkv, lam_params, subln, lam_init):
    b, nblk, _, s, _ = qkv.shape
    heads = nblk // 3
    tq = 256
    return pl.pallas_call(
        functools.partial(_diff_attn_kernel, lam_init=lam_init),
        grid=(b, heads, s // tq),
        in_specs=[
            pl.BlockSpec((1, 1, 1, tq, LANES), lambda bi, h, i: (bi, h, 0, i, 0)),
            pl.BlockSpec((1, 1, 1, s, LANES), lambda bi, h, i: (bi, heads + h, 0, 0, 0)),
            pl.BlockSpec((1, 1, 1, s, LANES), lambda bi, h, i: (bi, 2 * heads + h, 0, 0, 0)),
            pl.BlockSpec(lam_params.shape, lambda bi, h, i: (0, 0)),
            pl.BlockSpec((1, LANES), lambda bi, h, i: (0, 0)),
        ],
        out_specs=pl.BlockSpec((1, 1, tq, LANES), lambda bi, h, i: (bi, h, i, 0)),
        out_shape=jax.ShapeDtypeStruct((b, heads, s, LANES), BF16),
        compiler_params=_params("parallel", "parallel", "parallel"),
        name="diff_attention",
    )(qkv, qkv, qkv, lam_params, subln.reshape(1, LANES))


LRU_PAD = 8
LRU_ROWS = 256


def _log_sigmoid(x):
    return jnp.minimum(x, 0.0) - jnp.log1p(jnp.exp(-jnp.abs(x)))


def _gelu_tanh(x):
    return 0.5 * x * (1.0 + jnp.tanh(math.sqrt(2.0 / math.pi) * (x + 0.044715 * (x * x * x))))


def _lru_kernel(gate_ref, u_ref, cw_ref, cb_ref, wbd_ref, bias_ref, lam_ref, o_ref,
                upad_ref, a_ref, b_ref, hs_ref):
    s, cw = u_ref.shape[1], u_ref.shape[2]
    zeros = jnp.zeros((LRU_PAD, cw), F32)
    upad_ref[0:LRU_PAD, :] = zeros
    upad_ref[LRU_PAD + s:LRU_PAD + s + LRU_PAD, :] = zeros

    def copy_step(i, carry):
        r0 = pl.multiple_of(i * LRU_ROWS, LRU_ROWS)
        upad_ref[pl.ds(LRU_PAD + r0, LRU_ROWS), :] = u_ref[0, pl.ds(r0, LRU_ROWS), :]
        return carry

    lax.fori_loop(0, s // LRU_ROWS, copy_step, 0)
    row = lax.broadcasted_iota(jnp.int32, (LRU_ROWS, 1), 0) % SUBLANES

    for dr in range(2):
        log_a_unit = LRU_C * _log_sigmoid(lam_ref[dr:dr + 1, :])

        def gate_step(i, carry, dr=dr, log_a_unit=log_a_unit):
            r0 = pl.multiple_of(i * LRU_ROWS, LRU_ROWS)
            uw = _halo_window(upad_ref, LRU_PAD + r0, LRU_ROWS)
            xc = cb_ref[dr:dr + 1, :]
            for j in range(LRU_CONV):
                off = (j - (LRU_CONV - 1)) if dr == 0 else ((LRU_CONV - 1) - j)
                xc = xc + cw_ref[dr, j:j + 1, :] * _shift_rows(uw, off)[SUBLANES:SUBLANES + LRU_ROWS]
            pre = jnp.dot(xc.astype(BF16), wbd_ref[dr, 0], preferred_element_type=F32) + bias_ref[0, dr:dr + 1, :]
            r = jax.nn.sigmoid(pre[:, :cw])
            ig = jax.nn.sigmoid(pre[:, cw:])
            log_a = r * log_a_unit
            a = jnp.exp(log_a)
            th = jnp.tanh(log_a)
            bv = jnp.sqrt(-2.0 * th / (1.0 - th)) * (ig * xc)
            for k in (1, 2, 4):
                off = -k if dr == 0 else k
                keep = (row >= k) if dr == 0 else (row < SUBLANES - k)
                a_sh = _shift_rows(a, off)
                b_sh = _shift_rows(bv, off)
                bv = jnp.where(keep, a * b_sh + bv, bv)
                a = jnp.where(keep, a * a_sh, a)
            a_ref[pl.ds(r0, LRU_ROWS), :] = a
            b_ref[pl.ds(r0, LRU_ROWS), :] = bv
            return carry

        lax.fori_loop(0, s // LRU_ROWS, gate_step, 0)
        n_tiles = s // SUBLANES

        def scan_step(i, carry, dr=dr):
            g = i if dr == 0 else n_tiles - 1 - i
            r0 = pl.multiple_of(g * SUBLANES, SUBLANES)
            hcur = b_ref[pl.ds(r0, SUBLANES), :] + a_ref[pl.ds(r0, SUBLANES), :] * carry
            if dr == 0:
                hs_ref[pl.ds(r0, SUBLANES), :] = hcur
                return hcur[SUBLANES - 1:SUBLANES, :]
            hs_ref[pl.ds(r0, SUBLANES), :] = hs_ref[pl.ds(r0, SUBLANES), :] + hcur
            return hcur[0:1, :]

        lax.fori_loop(0, n_tiles, scan_step, jnp.zeros((1, cw), F32), unroll=8)

    def out_step(i, carry):
        r0 = pl.multiple_of(i * LRU_ROWS, LRU_ROWS)
        y = _gelu_tanh(gate_ref[0, pl.ds(r0, LRU_ROWS), :]) * hs_ref[pl.ds(r0, LRU_ROWS), :]
        o_ref[0, pl.ds(r0, LRU_ROWS), :] = y.astype(BF16)
        return carry

    lax.fori_loop(0, s // LRU_ROWS, out_step, 0)


def _lru_core(gu, conv_w, conv_b, w_a, b_a, w_x, b_x, lam):
    b, s, c2 = gu.shape
    c = c2 // 2
    cw = 256
    ncb = c // cw
    per = cw // HEAD_DIM
    eye = jnp.eye(per, dtype=F32)

    def block_diag(w):
        w5 = w.reshape(2, ncb, per, HEAD_DIM, HEAD_DIM)
        return jnp.einsum('dcipq,ik->dcipkq', w5, eye).reshape(2, ncb, cw, cw)

    wbd = jnp.concatenate([block_diag(w_a), block_diag(w_x)], axis=-1).astype(BF16)
    bias = jnp.concatenate([b_a.reshape(2, ncb, cw), b_x.reshape(2, ncb, cw)], axis=-1)
    bias = bias.transpose(1, 0, 2)
    return pl.pallas_call(
        _lru_kernel,
        grid=(b, ncb),
        in_specs=[
            pl.BlockSpec((1, s, cw), lambda bi, cb: (bi, 0, cb)),
            pl.BlockSpec((1, s, cw), lambda bi, cb: (bi, 0, ncb + cb)),
            pl.BlockSpec((2, LRU_CONV, cw), lambda bi, cb: (0, 0, cb)),
            pl.BlockSpec((2, cw), lambda bi, cb: (0, cb)),
            pl.BlockSpec((2, 1, cw, 2 * cw), lambda bi, cb: (0, cb, 0, 0)),
            pl.BlockSpec((1, 2, 2 * cw), lambda bi, cb: (cb, 0, 0)),
            pl.BlockSpec((2, cw), lambda bi, cb: (0, cb)),
        ],
        out_specs=pl.BlockSpec((1, s, cw), lambda bi, cb: (bi, 0, cb)),
        out_shape=jax.ShapeDtypeStruct((b, s, c), BF16),
        scratch_shapes=[
            pltpu.VMEM((s + 2 * LRU_PAD, cw), F32),
            pltpu.VMEM((s, cw), F32),
            pltpu.VMEM((s, cw), F32),
            pltpu.VMEM((s, cw), F32),
        ],
        compiler_params=_params("parallel", "parallel"),
        name="lru_core",
    )(gu, gu, conv_w, conv_b, wbd, bias, lam)


DIL_TQ = 128


def _band_scores(q, k, v, qpos0, kpos0, half):
    tq, w = q.shape[0], k.shape[0]
    lane = lax.broadcasted_iota(jnp.int32, (1, LANES), 1)
    first = lane < HEAD_DIM
    zero = jnp.zeros_like(q)
    qq = jnp.concatenate([jnp.where(first, q, zero), jnp.where(first, zero, q)], axis=0)
    sc = lax.dot_general(qq, k, (((1,), (1,)), ((), ())), preferred_element_type=F32)
    qpos = qpos0 + lax.broadcasted_iota(jnp.int32, (2 * tq, 1), 0) % tq
    kpos = kpos0 + lax.broadcasted_iota(jnp.int32, (1, w), 1)
    sc = jnp.where(jnp.abs(kpos - qpos) <= half, sc, NEG_INF)
    m = jnp.max(sc, axis=-1, keepdims=True)
    p = jnp.exp(sc - m)
    l = jnp.sum(p, axis=-1, keepdims=True)
    pv = jnp.dot(p.astype(BF16), v, preferred_element_type=F32)
    pick = lambda t: jnp.where(first, t[:tq], t[tq:])
    return pick(jnp.broadcast_to(m, (2 * tq, LANES))), pick(jnp.broadcast_to(l, (2 * tq, LANES))), pick(pv)


def _dilated_attn_kernel(q0_ref, k0_ref, v0_ref, q1_ref, k1_ref, v1_ref, q2_ref, k2_ref, v2_ref,
                         o_ref, m_ref, l_ref, acc_ref, *, halves, dils):
    s = o_ref.shape[2]
    tq = DIL_TQ
    win = 2 * tq

    def merge(rows, m_new, l_new, acc_new):
        m_old, l_old, acc_old = m_ref[rows, :], l_ref[rows, :], acc_ref[rows, :]
        m_tot = jnp.maximum(m_old, m_new)
        alpha = jnp.exp(m_old - m_tot)
        beta = jnp.exp(m_new - m_tot)
        m_ref[rows, :] = m_tot
        l_ref[rows, :] = alpha * l_old + beta * l_new
        acc_ref[rows, :] = alpha * acc_old + beta * acc_new

    def g0_step(i, carry):
        q0 = pl.multiple_of(i * tq, tq)
        ks = pl.multiple_of(jnp.clip(q0 - tq // 2, 0, s - win), tq // 2)
        m_new, l_new, acc_new = _band_scores(q0_ref[0, 0, 0, pl.ds(q0, tq), :],
                                             k0_ref[0, 0, 0, pl.ds(ks, win), :],
                                             v0_ref[0, 0, 0, pl.ds(ks, win), :], q0, ks, halves[0])
        m_ref[pl.ds(q0, tq), :] = m_new
        l_ref[pl.ds(q0, tq), :] = l_new
        acc_ref[pl.ds(q0, tq), :] = acc_new
        return carry

    lax.fori_loop(0, s // tq, g0_step, 0)

    for (q_ref, k_ref, v_ref, half, d) in ((q1_ref, k1_ref, v1_ref, halves[1], dils[1]),
                                           (q2_ref, k2_ref, v2_ref, halves[2], dils[2])):
        n = s // d
        tiles = n // tq
        w = min(win, n)

        def chain_step(c, carry, q_ref=q_ref, k_ref=k_ref, v_ref=v_ref, half=half, d=d, n=n, tiles=tiles, w=w):
            r = c // tiles
            i = c % tiles
            q0 = pl.multiple_of(i * tq, tq)
            ks = pl.multiple_of(jnp.clip(q0 - tq // 2, 0, n - w), tq // 2)
            m_new, l_new, acc_new = _band_scores(q_ref[0, 0, r, pl.ds(q0, tq), :],
                                                 k_ref[0, 0, r, pl.ds(ks, w), :],
                                                 v_ref[0, 0, r, pl.ds(ks, w), :], q0, ks, half)
            merge(pl.ds(q0 * d + r, tq, stride=d), m_new, l_new, acc_new)
            return carry

        lax.fori_loop(0, d * tiles, chain_step, 0)

    def out_step(i, carry):
        r0 = pl.multiple_of(i * 256, 256)
        o_ref[0, 0, pl.ds(r0, 256), :] = (acc_ref[pl.ds(r0, 256), :] / l_ref[pl.ds(r0, 256), :]).astype(BF16)
        return carry

    lax.fori_loop(0, s // 256, out_step, 0)


def _dilated_attention(qkv_groups):
    b, nblk, _, s, _ = qkv_groups[0].shape
    pairs = nblk // 3
    halves = tuple(w // (2 * d) for (w, d) in DIL_GROUPS)
    dils = tuple(d for (_, d) in DIL_GROUPS)
    in_specs, args = [], []
    for g, arr in enumerate(qkv_groups):
        d, n = arr.shape[2], arr.shape[3]
        for part in range(3):
            in_specs.append(pl.BlockSpec((1, 1, d, n, LANES),
                                         lambda bi, p, part=part: (bi, part * pairs + p, 0, 0, 0)))
            args.append(arr)
    return pl.pallas_call(
        functools.partial(_dilated_attn_kernel, halves=halves, dils=dils),
        grid=(b, pairs),
        in_specs=in_specs,
        out_specs=pl.BlockSpec((1, 1, s, LANES), lambda bi, p: (bi, p, 0, 0)),
        out_shape=jax.ShapeDtypeStruct((b, pairs, s, LANES), BF16),
        scratch_shapes=[pltpu.VMEM((s, LANES), F32), pltpu.VMEM((s, LANES), F32), pltpu.VMEM((s, LANES), F32)],
        compiler_params=_params("parallel", "parallel"),
        name="dilated_attention",
    )(*args)


FFN_PAD = 8
FFN_ROWS = 512
FFN_CHUNK = 256


def _ffn_kernel(h_ref, g_ref, wg_ref, wu_ref, cw_ref, cb_ref, wd_ref, fg_ref, o_ref, xn_ref, gp_ref,
                *, final_norm):
    j = pl.program_id(1)
    s = h_ref.shape[1]
    n_steps = s // FFN_ROWS

    @pl.when(j == 0)
    def _():
        zeros = jnp.zeros((FFN_PAD, gp_ref.shape[1]), F32)
        gp_ref[0:FFN_PAD, :] = zeros
        gp_ref[FFN_PAD + s:FFN_PAD + s + FFN_PAD, :] = zeros

        def norm_step(i, carry):
            r0 = pl.multiple_of(i * FFN_ROWS, FFN_ROWS)
            x = h_ref[0, pl.ds(r0, FFN_ROWS), :]
            xn_ref[pl.ds(r0, FFN_ROWS), :] = _rms_rows(x, g_ref[...], NORM_EPS).astype(BF16)
            o_ref[0, pl.ds(r0, FFN_ROWS), :] = x
            return carry

        lax.fori_loop(0, n_steps, norm_step, 0)

    def up_step(i, carry):
        r0 = pl.multiple_of(i * FFN_ROWS, FFN_ROWS)
        gp_ref[pl.ds(FFN_PAD + r0, FFN_ROWS), :] = jnp.dot(xn_ref[pl.ds(r0, FFN_ROWS), :], wg_ref[...],
                                                           preferred_element_type=F32)
        return carry

    lax.fori_loop(0, n_steps, up_step, 0)

    def down_step(i, carry):
        r0 = pl.multiple_of(i * FFN_ROWS, FFN_ROWS)
        gw = _halo_window(gp_ref, FFN_PAD + r0, FFN_ROWS)
        gc = cb_ref[...]
        for t in range(FFN_CONV):
            gc = gc + cw_ref[t:t + 1, :] * _shift_rows(gw, t - FFN_CONV // 2)[SUBLANES:SUBLANES + FFN_ROWS]
        act = 0.5 * gc * (1.0 + lax.erf(gc * (1.0 / math.sqrt(2.0))))
        u = jnp.dot(xn_ref[pl.ds(r0, FFN_ROWS), :], wu_ref[...], preferred_element_type=F32)
        hid = (act * u).astype(BF16)
        o_ref[0, pl.ds(r0, FFN_ROWS), :] += jnp.dot(hid, wd_ref[...], preferred_element_type=F32)
        return carry

    lax.fori_loop(0, n_steps, down_step, 0)

    if final_norm:
        @pl.when(j == pl.num_programs(1) - 1)
        def _():
            def fin_step(i, carry):
                r0 = pl.multiple_of(i * FFN_ROWS, FFN_ROWS)
                o_ref[0, pl.ds(r0, FFN_ROWS), :] = _rms_rows(o_ref[0, pl.ds(r0, FFN_ROWS), :], fg_ref[...], NORM_EPS)
                return carry

            lax.fori_loop(0, n_steps, fin_step, 0)


def _conv_ffn(h, gain, w_up, conv_w, conv_b, w_down, final_gain):
    b, s, dm = h.shape
    dff = w_down.shape[0]
    nj = dff // FFN_CHUNK
    assert dff % FFN_CHUNK == 0 and s % FFN_ROWS == 0
    final_norm = final_gain is not None
    fg = (final_gain if final_norm else gain).reshape(1, dm)
    return pl.pallas_call(
        functools.partial(_ffn_kernel, final_norm=final_norm),
        grid=(b, nj),
        in_specs=[
            pl.BlockSpec((1, s, dm), lambda bi, j: (bi, 0, 0)),
            pl.BlockSpec((1, dm), lambda bi, j: (0, 0)),
            pl.BlockSpec((dm, FFN_CHUNK), lambda bi, j: (0, j)),
            pl.BlockSpec((dm, FFN_CHUNK), lambda bi, j: (0, nj + j)),
            pl.BlockSpec((FFN_CONV, FFN_CHUNK), lambda bi, j: (0, j)),
            pl.BlockSpec((1, FFN_CHUNK), lambda bi, j: (0, j)),
            pl.BlockSpec((FFN_CHUNK, dm), lambda bi, j: (j, 0)),
            pl.BlockSpec((1, dm), lambda bi, j: (0, 0)),
        ],
        out_specs=pl.BlockSpec((1, s, dm), lambda bi, j: (bi, 0, 0)),
        out_shape=jax.ShapeDtypeStruct(h.shape, F32),
        scratch_shapes=[pltpu.VMEM((s, dm), BF16), pltpu.VMEM((s + 2 * FFN_PAD, FFN_CHUNK), F32)],
        compiler_params=_params("parallel", "arbitrary"),
        name="conv_ffn",
    )(h, gain.reshape(1, dm), w_up, w_up, conv_w, conv_b.reshape(1, dff), w_down, fg)


def kernel(x, positions, mix_norm, pool_w, pool_scale, diff_w_qkv, diff_lam_q1, diff_lam_k1, diff_lam_q2,
           diff_lam_k2, diff_subln, diff_w_o, lru_w_in, lru_conv_w, lru_conv_b, lru_w_a, lru_b_a, lru_w_x,
           lru_b_x, lru_lambda, lru_w_out, dil_w_qkv, dil_w_o, ffn_norm, ffn_w_up, ffn_conv_w, ffn_conv_b,
           ffn_w_down, final_norm):
    b, s, dm = x.shape
    depth = mix_norm.shape[0]
    cos, sin = _rope_tables(positions)
    h = x
    for i in range(depth):
        m, j = i % N_MIXERS, i // N_MIXERS
        if m == 0:
            h = _pool_mixer(h, mix_norm[i], pool_w[j], pool_scale[j])
        elif m == 1:
            qkv = _qkv_proj(h, mix_norm[i], diff_w_qkv[j].astype(BF16), 0, 3 * dm, cos, sin, 1)
            lam_params = jnp.zeros((SUBLANES, LANES), F32)
            lam_params = lam_params.at[0:4, 0:HEAD_DIM].set(
                jnp.stack([diff_lam_q1[j], diff_lam_k1[j], diff_lam_q2[j], diff_lam_k2[j]]).astype(F32))
            lam_init = 0.8 - 0.6 * math.exp(-0.3 * i)
            o = _diff_attention(qkv, lam_params, diff_subln[j], lam_init)
            h = _out_proj_hm(o, diff_w_o[j].astype(BF16), h)
        elif m == 2:
            gu = _norm_proj(h.reshape(b * s, dm), mix_norm[i], lru_w_in[j].astype(BF16)).reshape(b, s, -1)
            y = _lru_core(gu, lru_conv_w[j], lru_conv_b[j], lru_w_a[j], lru_b_a[j], lru_w_x[j], lru_b_x[j],
                          lru_lambda[j])
            h = _out_proj(y, lru_w_out[j].astype(BF16), h)
        else:
            w = dil_w_qkv[j].astype(BF16)
            groups = [_qkv_proj(h, mix_norm[i], w, g * 3 * dm, 3 * dm, cos, sin, d)
                      for g, (_, d) in enumerate(DIL_GROUPS)]
            o = _dilated_attention(groups)
            h = _out_proj_hm(o, dil_w_o[j].astype(BF16), h)
        h = _conv_ffn(h, ffn_norm[i], ffn_w_up[i].astype(BF16), ffn_conv_w[i], ffn_conv_b[i],
                      ffn_w_down[i].astype(BF16), final_norm if i == depth - 1 else None)
    return h
```

```python
import functools
import math

import jax
import jax.numpy as jnp
from jax import lax
from jax.experimental import pallas as pl
from jax.experimental.pallas import tpu as pltpu

F32 = jnp.float32
BF16 = jnp.bfloat16

HEAD_DIM = 64
NORM_EPS = 1e-6
ROPE_THETA = 10000.0
NEG_INF = -1e30
POOL_WINDOWS = (2, 4, 8, 16)
DIFF_SUBLN_EPS = 1e-5
LRU_C = 8.0
LRU_CONV = 4
DIL_GROUPS = ((128, 1), (512, 4), (2048, 16))
FFN_CONV = 3
N_MIXERS = 4

LANES = 128
SUBLANES = 8
VMEM_LIMIT_BYTES = 56 * 1024 * 1024


def _params(*semantics):
    return pltpu.CompilerParams(dimension_semantics=semantics, vmem_limit_bytes=VMEM_LIMIT_BYTES)


def _rms_rows(x, gain, eps):
    ms = jnp.mean(x * x, axis=-1, keepdims=True)
    return x * lax.rsqrt(ms + eps) * gain


def _halo_window(ref, r0, rows, cols=slice(None)):
    return ref[pl.ds(r0 - SUBLANES, rows + 2 * SUBLANES), cols]


def _shift_rows(win, off):
    if off == 0:
        return win
    return pltpu.roll(win, (-off) % win.shape[0], 0)


def _rope_table_kernel(pos_ref, inv_ref, cos_ref, sin_ref):
    ang = pos_ref[...] * inv_ref[...]
    lane = lax.broadcasted_iota(jnp.int32, ang.shape, 1)
    first_half = (lane % HEAD_DIM) < (HEAD_DIM // 2)
    cos_ref[...] = jnp.cos(ang)
    sin_ref[...] = jnp.where(first_half, -jnp.sin(ang), jnp.sin(ang))


def _rope_tables(positions):
    s = positions.shape[0]
    inv = ROPE_THETA ** (-jnp.arange(0, HEAD_DIM, 2, dtype=F32) / HEAD_DIM)
    inv = jnp.tile(inv, LANES // (HEAD_DIM // 2)).reshape(1, LANES)
    pos = positions.astype(F32).reshape(s, 1)
    return pl.pallas_call(
        _rope_table_kernel,
        out_shape=(jax.ShapeDtypeStruct((s, LANES), F32), jax.ShapeDtypeStruct((s, LANES), F32)),
        name="rope_tables",
    )(pos, inv)


def _qkv_proj_kernel(x_ref, g_ref, w_ref, cos_ref, sin_ref, o_ref, xn_ref, *, dsub, n_tile, d_model):
    j = pl.program_id(3)
    n_chunks = w_ref.shape[1] // LANES

    @pl.when(j == 0)
    def _():
        for r in range(dsub):
            x = x_ref[0, :, r * d_model:(r + 1) * d_model]
            xn_ref[r * n_tile:(r + 1) * n_tile, :] = _rms_rows(x, g_ref[...], NORM_EPS).astype(BF16)

    y = jnp.dot(xn_ref[...], w_ref[...], preferred_element_type=F32)
    kind = j % 3

    def store(c, val):
        for r in range(dsub):
            o_ref[0, c, r] = val[r * n_tile:(r + 1) * n_tile, :].astype(BF16)

    @pl.when(kind == 2)
    def _():
        for c in range(n_chunks):
            store(c, y[:, c * LANES:(c + 1) * LANES])

    def rope_store(scale):
        if dsub == 1:
            ct, st = cos_ref[...], sin_ref[...]
        else:
            ct = jnp.concatenate([cos_ref[:, r * LANES:(r + 1) * LANES] for r in range(dsub)], axis=0)
            st = jnp.concatenate([sin_ref[:, r * LANES:(r + 1) * LANES] for r in range(dsub)], axis=0)
        lane = lax.broadcasted_iota(jnp.int32, (1, LANES), 1)
        first_half = (lane % HEAD_DIM) < (HEAD_DIM // 2)
        for c in range(n_chunks):
            yc = y[:, c * LANES:(c + 1) * LANES]
            partner = jnp.where(first_half, pltpu.roll(yc, LANES - HEAD_DIM // 2, 1),
                                pltpu.roll(yc, HEAD_DIM // 2, 1))
            out = yc * ct + partner * st
            if scale != 1.0:
                out = out * scale
            store(c, out)

    @pl.when(kind == 0)
    def _():
        rope_store(HEAD_DIM ** -0.5)

    @pl.when(kind == 1)
    def _():
        rope_store(1.0)


def _qkv_proj(h, gain, w, col0, n_cols, cos, sin, dilation):
    b, s, dm = h.shape
    d = dilation
    n = s // d
    tm = 1024
    tn = 1024
    dsub = min(d, 8)
    n_tile = tm // dsub
    if n_tile > n:
        n_tile, dsub = n, tm // n
    assert n % n_tile == 0 and d % dsub == 0 and n_cols % tn == 0 and col0 % tn == 0
    xv = h.reshape(b, n, d * dm)
    cv = cos.reshape(n, d * LANES)
    sv = sin.reshape(n, d * LANES)
    jt0 = col0 // tn
    grid = (b, d // dsub, n // n_tile, n_cols // tn)
    return pl.pallas_call(
        functools.partial(_qkv_proj_kernel, dsub=dsub, n_tile=n_tile, d_model=dm),
        grid=grid,
        in_specs=[
            pl.BlockSpec((1, n_tile, dsub * dm), lambda bi, rb, i, j: (bi, i, rb)),
            pl.BlockSpec((1, dm), lambda bi, rb, i, j: (0, 0)),
            pl.BlockSpec((dm, tn), lambda bi, rb, i, j: (0, jt0 + j)),
            pl.BlockSpec((n_tile, dsub * LANES), lambda bi, rb, i, j: (i, rb)),
            pl.BlockSpec((n_tile, dsub * LANES), lambda bi, rb, i, j: (i, rb)),
        ],
        out_specs=pl.BlockSpec((1, tn // LANES, dsub, n_tile, LANES), lambda bi, rb, i, j: (bi, j, rb, i, 0)),
        out_shape=jax.ShapeDtypeStruct((b, n_cols // LANES, d, n, LANES), BF16),
        scratch_shapes=[pltpu.VMEM((tm, dm), BF16)],
        compiler_params=_params("parallel", "parallel", "parallel", "arbitrary"),
        name=f"qkv_proj_d{d}",
    )(xv, gain.reshape(1, dm), w, cv, sv)


def _norm_proj_kernel(x_ref, g_ref, w_ref, o_ref, xn_ref):
    @pl.when(pl.program_id(1) == 0)
    def _():
        xn_ref[...] = _rms_rows(x_ref[...], g_ref[...], NORM_EPS).astype(BF16)

    o_ref[...] = jnp.dot(xn_ref[...], w_ref[...], preferred_element_type=F32)


def _norm_proj(h2d, gain, w):
    m, dm = h2d.shape
    n = w.shape[1]
    tm, tn = 1024, 1024
    assert m % tm == 0 and n % tn == 0
    return pl.pallas_call(
        _norm_proj_kernel,
        grid=(m // tm, n // tn),
        in_specs=[
            pl.BlockSpec((tm, dm), lambda i, j: (i, 0)),
            pl.BlockSpec((1, dm), lambda i, j: (0, 0)),
            pl.BlockSpec((dm, tn), lambda i, j: (0, j)),
        ],
        out_specs=pl.BlockSpec((tm, tn), lambda i, j: (i, j)),
        out_shape=jax.ShapeDtypeStruct((m, n), F32),
        scratch_shapes=[pltpu.VMEM((tm, dm), BF16)],
        compiler_params=_params("parallel", "arbitrary"),
        name="norm_proj",
    )(h2d, gain.reshape(1, dm), w)


def _out_proj_hm_kernel(a_ref, w_ref, h_ref, o_ref):
    a = jnp.concatenate([a_ref[0, p] for p in range(a_ref.shape[1])], axis=-1)
    o_ref[0] = h_ref[0] + jnp.dot(a, w_ref[...], preferred_element_type=F32)


def _out_proj_hm(a_hm, w, h):
    b, s, dm = h.shape
    kb = a_hm.shape[1]
    tm = 1024
    return pl.pallas_call(
        _out_proj_hm_kernel,
        grid=(b, s // tm),
        in_specs=[
            pl.BlockSpec((1, kb, tm, LANES), lambda bi, i: (bi, 0, i, 0)),
            pl.BlockSpec(w.shape, lambda bi, i: (0, 0)),
            pl.BlockSpec((1, tm, dm), lambda bi, i: (bi, i, 0)),
        ],
        out_specs=pl.BlockSpec((1, tm, dm), lambda bi, i: (bi, i, 0)),
        out_shape=jax.ShapeDtypeStruct(h.shape, F32),
        compiler_params=_params("parallel", "parallel"),
        name="out_proj_hm",
    )(a_hm, w, h)


def _out_proj_kernel(a_ref, w_ref, h_ref, o_ref):
    o_ref[0] = h_ref[0] + jnp.dot(a_ref[0], w_ref[...], preferred_element_type=F32)


def _out_proj(a, w, h):
    b, s, dm = h.shape
    k = a.shape[-1]
    tm = 1024
    return pl.pallas_call(
        _out_proj_kernel,
        grid=(b, s // tm),
        in_specs=[
            pl.BlockSpec((1, tm, k), lambda bi, i: (bi, i, 0)),
            pl.BlockSpec(w.shape, lambda bi, i: (0, 0)),
            pl.BlockSpec((1, tm, dm), lambda bi, i: (bi, i, 0)),
        ],
        out_specs=pl.BlockSpec((1, tm, dm), lambda bi, i: (bi, i, 0)),
        out_shape=jax.ShapeDtypeStruct(h.shape, F32),
        compiler_params=_params("parallel", "parallel"),
        name="out_proj",
    )(a, w, h)


POOL_PAD = 8
POOL_ROWS = 256


def _pool_kernel(h_ref, g_ref, w_ref, sc_ref, o_ref, hn_ref):
    s, dm = h_ref.shape[1], h_ref.shape[2]
    group = dm // len(POOL_WINDOWS)
    zeros = jnp.zeros((POOL_PAD, dm), F32)
    hn_ref[0:POOL_PAD, :] = zeros
    hn_ref[POOL_PAD + s:POOL_PAD + s + POOL_PAD, :] = zeros

    def norm_step(i, carry):
        r0 = pl.multiple_of(i * POOL_ROWS, POOL_ROWS)
        hn_ref[pl.ds(POOL_PAD + r0, POOL_ROWS), :] = _rms_rows(h_ref[0, pl.ds(r0, POOL_ROWS), :], g_ref[...], NORM_EPS)
        return carry

    lax.fori_loop(0, s // POOL_ROWS, norm_step, 0)

    def mix_step(i, carry):
        r0 = pl.multiple_of(i * POOL_ROWS, POOL_ROWS)
        t = r0 + lax.broadcasted_iota(jnp.int32, (POOL_ROWS, 1), 0)
        inner = slice(SUBLANES, SUBLANES + POOL_ROWS)
        for g, win in enumerate(POOL_WINDOWS):
            half = win // 2
            sl = slice(g * group, (g + 1) * group)
            xw = _halo_window(hn_ref, POOL_PAD + r0, POOL_ROWS, sl)
            tot = xw + _shift_rows(xw, -1)
            step = 1
            while 2 * step < win:
                tot = _shift_rows(tot, -step) + _shift_rows(tot, step)
                step *= 2
            cnt = (jnp.minimum(t + half, s) - jnp.maximum(t - half, 0)).astype(F32)
            pooled = (tot[inner] / cnt - xw[inner]).astype(BF16)
            y = jnp.dot(pooled, w_ref[g], preferred_element_type=F32) * sc_ref[:, sl]
            o_ref[0, pl.ds(r0, POOL_ROWS), sl] = h_ref[0, pl.ds(r0, POOL_ROWS), sl] + y
        return carry

    lax.fori_loop(0, s // POOL_ROWS, mix_step, 0)


def _pool_mixer(h, gain, w, scale):
    b, s, dm = h.shape
    return pl.pallas_call(
        _pool_kernel,
        grid=(b,),
        in_specs=[
            pl.BlockSpec((1, s, dm), lambda bi: (bi, 0, 0)),
            pl.BlockSpec((1, dm), lambda bi: (0, 0)),
            pl.BlockSpec(w.shape, lambda bi: (0, 0, 0)),
            pl.BlockSpec((1, dm), lambda bi: (0, 0)),
        ],
        out_specs=pl.BlockSpec((1, s, dm), lambda bi: (bi, 0, 0)),
        out_shape=jax.ShapeDtypeStruct(h.shape, F32),
        scratch_shapes=[pltpu.VMEM((s + 2 * POOL_PAD, dm), F32)],
        compiler_params=_params("parallel"),
        name="pool_mixer",
    )(h, gain.reshape(1, dm), w.astype(BF16), scale.reshape(1, dm))


def _diff_attn_kernel(q_ref, k_ref, v_ref, lam_ref, sub_ref, o_ref, *, lam_init):
    q = q_ref[0, 0, 0]
    k = k_ref[0, 0, 0]
    v = v_ref[0, 0, 0]
    lp = lam_ref[...]
    lam = (jnp.exp(jnp.sum(lp[0:1] * lp[1:2], axis=-1, keepdims=True))
           - jnp.exp(jnp.sum(lp[2:3] * lp[3:4], axis=-1, keepdims=True)) + lam_init)
    lane = lax.broadcasted_iota(jnp.int32, (1, LANES), 1)
    first = lane < HEAD_DIM
    zero = jnp.zeros_like(q)
    dn = (((1,), (1,)), ((), ()))

    def softmax_parts(qm):
        sc = lax.dot_general(qm, k, dn, preferred_element_type=F32)
        p = jnp.exp(sc - jnp.max(sc, axis=-1, keepdims=True))
        return p, jnp.sum(p, axis=-1, keepdims=True)

    p0, l0 = softmax_parts(jnp.where(first, q, zero))
    p1, l1 = softmax_parts(jnp.where(first, zero, q))
    a = p0 * (1.0 / l0) - p1 * (lam / l1)
    o = jnp.dot(a.astype(BF16), v, preferred_element_type=F32)
    o = _rms_rows(o, sub_ref[...], DIFF_SUBLN_EPS) * (1.0 - lam_init)
    o_ref[0, 0] = o.astype(BF16)


def _diff_attention(qkv, lam_params, subln, lam_init):
    b, nblk, _, s, _ = qkv.shape
    heads = nblk // 3
    tq = 256
    return pl.pallas_call(
        functools.partial(_diff_attn_kernel, lam_init=lam_init),
        grid=(b, heads, s // tq),
        in_specs=[
            pl.BlockSpec((1, 1, 1, tq, LANES), lambda bi, h, i: (bi, h, 0, i, 0)),
            pl.BlockSpec((1, 1, 1, s, LANES), lambda bi, h, i: (bi, heads + h, 0, 0, 0)),
            pl.BlockSpec((1, 1, 1, s, LANES), lambda bi, h, i: (bi, 2 * heads + h, 0, 0, 0)),
            pl.BlockSpec(lam_params.shape, lambda bi, h, i: (0, 0)),
            pl.BlockSpec((1, LANES), lambda bi, h, i: (0, 0)),
        ],
        out_specs=pl.BlockSpec((1, 1, tq, LANES), lambda bi, h, i: (bi, h, i, 0)),
        out_shape=jax.ShapeDtypeStruct((b, heads, s, LANES), BF16),
        compiler_params=_params("parallel", "parallel", "parallel"),
        name="diff_attention",
    )(qkv, qkv, qkv, lam_params, subln.reshape(1, LANES))


LRU_PAD = 8
LRU_ROWS = 256


def _log_sigmoid(x):
    return jnp.minimum(x, 0.0) - jnp.log1p(jnp.exp(-jnp.abs(x)))


def _gelu_tanh(x):
    return 0.5 * x * (1.0 + jnp.tanh(math.sqrt(2.0 / math.pi) * (x + 0.044715 * (x * x * x))))


def _lru_kernel(gate_ref, u_ref, cw_ref, cb_ref, wbd_ref, bias_ref, lam_ref, o_ref,
                upad_ref, a_ref, b_ref, hs_ref):
    s, cw = u_ref.shape[1], u_ref.shape[2]
    zeros = jnp.zeros((LRU_PAD, cw), F32)
    upad_ref[0:LRU_PAD, :] = zeros
    upad_ref[LRU_PAD + s:LRU_PAD + s + LRU_PAD, :] = zeros

    def copy_step(i, carry):
        r0 = pl.multiple_of(i * LRU_ROWS, LRU_ROWS)
        upad_ref[pl.ds(LRU_PAD + r0, LRU_ROWS), :] = u_ref[0, pl.ds(r0, LRU_ROWS), :]
        return carry

    lax.fori_loop(0, s // LRU_ROWS, copy_step, 0)
    row = lax.broadcasted_iota(jnp.int32, (LRU_ROWS, 1), 0) % SUBLANES

    for dr in range(2):
        log_a_unit = LRU_C * _log_sigmoid(lam_ref[dr:dr + 1, :])

        def gate_step(i, carry, dr=dr, log_a_unit=log_a_unit):
            r0 = pl.multiple_of(i * LRU_ROWS, LRU_ROWS)
            uw = _halo_window(upad_ref, LRU_PAD + r0, LRU_ROWS)
            xc = cb_ref[dr:dr + 1, :]
            for j in range(LRU_CONV):
                off = (j - (LRU_CONV - 1)) if dr == 0 else ((LRU_CONV - 1) - j)
                xc = xc + cw_ref[dr, j:j + 1, :] * _shift_rows(uw, off)[SUBLANES:SUBLANES + LRU_ROWS]
            pre = jnp.dot(xc.astype(BF16), wbd_ref[dr, 0], preferred_element_type=F32) + bias_ref[0, dr:dr + 1, :]
            r = jax.nn.sigmoid(pre[:, :cw])
            ig = jax.nn.sigmoid(pre[:, cw:])
            log_a = r * log_a_unit
            a = jnp.exp(log_a)
            th = jnp.tanh(log_a)
            bv = jnp.sqrt(-2.0 * th / (1.0 - th)) * (ig * xc)
            for k in (1, 2, 4):
                off = -k if dr == 0 else k
                keep = (row >= k) if dr == 0 else (row < SUBLANES - k)
                a_sh = _shift_rows(a, off)
                b_sh = _shift_rows(bv, off)
                bv = jnp.where(keep, a * b_sh + bv, bv)
                a = jnp.where(keep, a * a_sh, a)
            a_ref[pl.ds(r0, LRU_ROWS), :] = a
            b_ref[pl.ds(r0, LRU_ROWS), :] = bv
            return carry

        lax.fori_loop(0, s // LRU_ROWS, gate_step, 0)
        n_tiles = s // SUBLANES

        def scan_step(i, carry, dr=dr):
            g = i if dr == 0 else n_tiles - 1 - i
            r0 = pl.multiple_of(g * SUBLANES, SUBLANES)
            hcur = b_ref[pl.ds(r0, SUBLANES), :] + a_ref[pl.ds(r0, SUBLANES), :] * carry
            if dr == 0:
                hs_ref[pl.ds(r0, SUBLANES), :] = hcur
                return hcur[SUBLANES - 1:SUBLANES, :]
            hs_ref[pl.ds(r0, SUBLANES), :] = hs_ref[pl.ds(r0, SUBLANES), :] + hcur
            return hcur[0:1, :]

        lax.fori_loop(0, n_tiles, scan_step, jnp.zeros((1, cw), F32), unroll=8)

    def out_step(i, carry):
        r0 = pl.multiple_of(i * LRU_ROWS, LRU_ROWS)
        y = _gelu_tanh(gate_ref[0, pl.ds(r0, LRU_ROWS), :]) * hs_ref[pl.ds(r0, LRU_ROWS), :]
        o_ref[0, pl.ds(r0, LRU_ROWS), :] = y.astype(BF16)
        return carry

    lax.fori_loop(0, s // LRU_ROWS, out_step, 0)


def _lru_core(gu, conv_w, conv_b, w_a, b_a, w_x, b_x, lam):
    b, s, c2 = gu.shape
    c = c2 // 2
    cw = 256
    ncb = c // cw
    per = cw // HEAD_DIM
    eye = jnp.eye(per, dtype=F32)

    def block_diag(w):
        w5 = w.reshape(2, ncb, per, HEAD_DIM, HEAD_DIM)
        return jnp.einsum('dcipq,ik->dcipkq', w5, eye).reshape(2, ncb, cw, cw)

    wbd = jnp.concatenate([block_diag(w_a), block_diag(w_x)], axis=-1).astype(BF16)
    bias = jnp.concatenate([b_a.reshape(2, ncb, cw), b_x.reshape(2, ncb, cw)], axis=-1)
    bias = bias.transpose(1, 0, 2)
    return pl.pallas_call(
        _lru_kernel,
        grid=(b, ncb),
        in_specs=[
            pl.BlockSpec((1, s, cw), lambda bi, cb: (bi, 0, cb)),
            pl.BlockSpec((1, s, cw), lambda bi, cb: (bi, 0, ncb + cb)),
            pl.BlockSpec((2, LRU_CONV, cw), lambda bi, cb: (0, 0, cb)),
            pl.BlockSpec((2, cw), lambda bi, cb: (0, cb)),
            pl.BlockSpec((2, 1, cw, 2 * cw), lambda bi, cb: (0, cb, 0, 0)),
            pl.BlockSpec((1, 2, 2 * cw), lambda bi, cb: (cb, 0, 0)),
            pl.BlockSpec((2, cw), lambda bi, cb: (0, cb)),
        ],
        out_specs=pl.BlockSpec((1, s, cw), lambda bi, cb: (bi, 0, cb)),
        out_shape=jax.ShapeDtypeStruct((b, s, c), BF16),
        scratch_shapes=[
            pltpu.VMEM((s + 2 * LRU_PAD, cw), F32),
            pltpu.VMEM((s, cw), F32),
            pltpu.VMEM((s, cw), F32),
            pltpu.VMEM((s, cw), F32),
        ],
        compiler_params=_params("parallel", "parallel"),
        name="lru_core",
    )(gu, gu, conv_w, conv_b, wbd, bias, lam)


DIL_TQ = 128
DIL_UNROLL = 4


def _band_scores(q, k, v, qpos0, kpos0, half):
    tq, w = q.shape[0], k.shape[0]
    lane = lax.broadcasted_iota(jnp.int32, (1, LANES), 1)
    first = lane < HEAD_DIM
    zero = jnp.zeros_like(q)
    qq = jnp.concatenate([jnp.where(first, q, zero), jnp.where(first, zero, q)], axis=0)
    sc = lax.dot_general(qq, k, (((1,), (1,)), ((), ())), preferred_element_type=F32)
    qpos = qpos0 + lax.broadcasted_iota(jnp.int32, (2 * tq, 1), 0) % tq
    kpos = kpos0 + lax.broadcasted_iota(jnp.int32, (1, w), 1)
    sc = jnp.where(jnp.abs(kpos - qpos) <= half, sc, NEG_INF)
    m = jnp.max(sc, axis=-1, keepdims=True)
    p = jnp.exp(sc - m)
    l = jnp.sum(p, axis=-1, keepdims=True)
    pv = jnp.dot(p.astype(BF16), v, preferred_element_type=F32)
    pick = lambda t: jnp.where(first, t[:tq], t[tq:])
    return pick(jnp.broadcast_to(m, (2 * tq, LANES))), pick(jnp.broadcast_to(l, (2 * tq, LANES))), pick(pv)


def _dilated_attn_kernel(q0_ref, k0_ref, v0_ref, q1_ref, k1_ref, v1_ref, q2_ref, k2_ref, v2_ref,
                         o_ref, m_ref, l_ref, acc_ref, *, halves, dils):
    s = o_ref.shape[2]
    tq = DIL_TQ
    win = 2 * tq

    def merge(rows, m_new, l_new, acc_new):
        m_old, l_old, acc_old = m_ref[rows, :], l_ref[rows, :], acc_ref[rows, :]
        m_tot = jnp.maximum(m_old, m_new)
        alpha = jnp.exp(m_old - m_tot)
        beta = jnp.exp(m_new - m_tot)
        m_ref[rows, :] = m_tot
        l_ref[rows, :] = alpha * l_old + beta * l_new
        acc_ref[rows, :] = alpha * acc_old + beta * acc_new

    def g0_step(i, carry):
        q0 = pl.multiple_of(i * tq, tq)
        ks = pl.multiple_of(jnp.clip(q0 - tq // 2, 0, s - win), tq // 2)
        m_new, l_new, acc_new = _band_scores(q0_ref[0, 0, 0, pl.ds(q0, tq), :],
                                             k0_ref[0, 0, 0, pl.ds(ks, win), :],
                                             v0_ref[0, 0, 0, pl.ds(ks, win), :], q0, ks, halves[0])
        m_ref[pl.ds(q0, tq), :] = m_new
        l_ref[pl.ds(q0, tq), :] = l_new
        acc_ref[pl.ds(q0, tq), :] = acc_new
        return carry

    lax.fori_loop(0, s // tq, g0_step, 0, unroll=DIL_UNROLL)

    for (q_ref, k_ref, v_ref, half, d) in ((q1_ref, k1_ref, v1_ref, halves[1], dils[1]),
                                           (q2_ref, k2_ref, v2_ref, halves[2], dils[2])):
        n = s // d
        tiles = n // tq
        w = min(win, n)

        def chain_step(c, carry, q_ref=q_ref, k_ref=k_ref, v_ref=v_ref, half=half, d=d, n=n, tiles=tiles, w=w):
            r = c // tiles
            i = c % tiles
            q0 = pl.multiple_of(i * tq, tq)
            ks = pl.multiple_of(jnp.clip(q0 - tq // 2, 0, n - w), tq // 2)
            m_new, l_new, acc_new = _band_scores(q_ref[0, 0, r, pl.ds(q0, tq), :],
                                                 k_ref[0, 0, r, pl.ds(ks, w), :],
                                                 v_ref[0, 0, r, pl.ds(ks, w), :], q0, ks, half)
            merge(pl.ds(q0 * d + r, tq, stride=d), m_new, l_new, acc_new)
            return carry

        lax.fori_loop(0, d * tiles, chain_step, 0, unroll=DIL_UNROLL)

    def out_step(i, carry):
        r0 = pl.multiple_of(i * 256, 256)
        o_ref[0, 0, pl.ds(r0, 256), :] = (acc_ref[pl.ds(r0, 256), :] / l_ref[pl.ds(r0, 256), :]).astype(BF16)
        return carry

    lax.fori_loop(0, s // 256, out_step, 0)


def _dilated_attention(qkv_groups):
    b, nblk, _, s, _ = qkv_groups[0].shape
    pairs = nblk // 3
    halves = tuple(w // (2 * d) for (w, d) in DIL_GROUPS)
    dils = tuple(d for (_, d) in DIL_GROUPS)
    in_specs, args = [], []
    for g, arr in enumerate(qkv_groups):
        d, n = arr.shape[2], arr.shape[3]
        for part in range(3):
            in_specs.append(pl.BlockSpec((1, 1, d, n, LANES),
                                         lambda bi, p, part=part: (bi, part * pairs + p, 0, 0, 0)))
            args.append(arr)
    return pl.pallas_call(
        functools.partial(_dilated_attn_kernel, halves=halves, dils=dils),
        grid=(b, pairs),
        in_specs=in_specs,
        out_specs=pl.BlockSpec((1, 1, s, LANES), lambda bi, p: (bi, p, 0, 0)),
        out_shape=jax.ShapeDtypeStruct((b, pairs, s, LANES), BF16),
        scratch_shapes=[pltpu.VMEM((s, LANES), F32), pltpu.VMEM((s, LANES), F32), pltpu.VMEM((s, LANES), F32)],
        compiler_params=_params("parallel", "parallel"),
        name="dilated_attention",
    )(*args)


FFN_HALO = 16
FFN_UP_ROWS = 1024
FFN_DOWN_ROWS = 512
FFN_CHUNK = 256


def _ffn_kernel(h_ref, g_ref, wgu_ref, cw_ref, cb_ref, wd_ref, hr_ref, fg_ref, o_ref, xn_ref, hid_ref,
                *, final_norm, n_up):
    j = pl.program_id(1)
    s = h_ref.shape[1]

    @pl.when(j == 0)
    def _():
        zeros = jnp.zeros((FFN_HALO, xn_ref.shape[1]), BF16)
        xn_ref[0:FFN_HALO, :] = zeros
        xn_ref[FFN_HALO + s:FFN_HALO + s + FFN_HALO, :] = zeros

        def norm_step(i, carry):
            r0 = pl.multiple_of(i * FFN_DOWN_ROWS, FFN_DOWN_ROWS)
            x = h_ref[0, pl.ds(r0, FFN_DOWN_ROWS), :]
            xn_ref[pl.ds(FFN_HALO + r0, FFN_DOWN_ROWS), :] = _rms_rows(x, g_ref[...], NORM_EPS).astype(BF16)
            return carry

        lax.fori_loop(0, s // FFN_DOWN_ROWS, norm_step, 0)

    @pl.when(j < n_up)
    def _():
        inner = slice(FFN_HALO, FFN_HALO + FFN_UP_ROWS)
        for rc in range(s // FFN_UP_ROWS):
            xs = xn_ref[rc * FFN_UP_ROWS:rc * FFN_UP_ROWS + FFN_UP_ROWS + 2 * FFN_HALO, :]
            gu = jnp.dot(xs, wgu_ref[...], preferred_element_type=F32)
            g = gu[:, :FFN_CHUNK]
            gc = cb_ref[...]
            for t in range(FFN_CONV):
                gc = gc + cw_ref[t:t + 1, :] * _shift_rows(g, t - FFN_CONV // 2)[inner]
            act = 0.5 * gc * (1.0 + lax.erf(gc * (1.0 / math.sqrt(2.0))))
            hid_ref[j, rc * FFN_UP_ROWS:(rc + 1) * FFN_UP_ROWS, :] = (act * gu[inner, FFN_CHUNK:]).astype(BF16)

    @pl.when(j >= n_up)
    def _():
        r0 = pl.multiple_of((j - n_up) * FFN_DOWN_ROWS, FFN_DOWN_ROWS)
        hid = jnp.concatenate([hid_ref[c, pl.ds(r0, FFN_DOWN_ROWS), :] for c in range(n_up)], axis=-1)
        out = hr_ref[0] + jnp.dot(hid, wd_ref[...], preferred_element_type=F32)
        if final_norm:
            out = _rms_rows(out, fg_ref[...], NORM_EPS)
        o_ref[0] = out


def _conv_ffn(h, gain, w_up, conv_w, conv_b, w_down, final_gain):
    b, s, dm = h.shape
    dff = w_down.shape[0]
    n_up = dff // FFN_CHUNK
    n_down = s // FFN_DOWN_ROWS
    assert dff % FFN_CHUNK == 0 and s % FFN_UP_ROWS == 0 and s % FFN_DOWN_ROWS == 0
    final_norm = final_gain is not None
    fg = (final_gain if final_norm else gain).reshape(1, dm)
    wgu = w_up.reshape(dm, 2, n_up, FFN_CHUNK).transpose(0, 2, 1, 3).reshape(dm, 2 * dff)
    up_idx = lambda j: jnp.minimum(j, n_up - 1)
    down_idx = lambda j: jnp.maximum(j - n_up, 0)
    return pl.pallas_call(
        functools.partial(_ffn_kernel, final_norm=final_norm, n_up=n_up),
        grid=(b, n_up + n_down),
        in_specs=[
            pl.BlockSpec((1, s, dm), lambda bi, j: (bi, 0, 0), pipeline_mode=pl.Buffered(1)),
            pl.BlockSpec((1, dm), lambda bi, j: (0, 0)),
            pl.BlockSpec((dm, 2 * FFN_CHUNK), lambda bi, j: (0, up_idx(j))),
            pl.BlockSpec((FFN_CONV, FFN_CHUNK), lambda bi, j: (0, up_idx(j))),
            pl.BlockSpec((1, FFN_CHUNK), lambda bi, j: (0, up_idx(j))),
            pl.BlockSpec((dff, dm), lambda bi, j: (0, 0), pipeline_mode=pl.Buffered(1)),
            pl.BlockSpec((1, FFN_DOWN_ROWS, dm), lambda bi, j: (bi, down_idx(j), 0)),
            pl.BlockSpec((1, dm), lambda bi, j: (0, 0)),
        ],
        out_specs=pl.BlockSpec((1, FFN_DOWN_ROWS, dm), lambda bi, j: (bi, down_idx(j), 0)),
        out_shape=jax.ShapeDtypeStruct(h.shape, F32),
        scratch_shapes=[pltpu.VMEM((s + 2 * FFN_HALO, dm), BF16), pltpu.VMEM((n_up, s, FFN_CHUNK), BF16)],
        compiler_params=_params("parallel", "arbitrary"),
        name="conv_ffn",
    )(h, gain.reshape(1, dm), wgu, conv_w, conv_b.reshape(1, dff), w_down, h, fg)


def kernel(x, positions, mix_norm, pool_w, pool_scale, diff_w_qkv, diff_lam_q1, diff_lam_k1, diff_lam_q2,
           diff_lam_k2, diff_subln, diff_w_o, lru_w_in, lru_conv_w, lru_conv_b, lru_w_a, lru_b_a, lru_w_x,
           lru_b_x, lru_lambda, lru_w_out, dil_w_qkv, dil_w_o, ffn_norm, ffn_w_up, ffn_conv_w, ffn_conv_b,
           ffn_w_down, final_norm):
    b, s, dm = x.shape
    depth = mix_norm.shape[0]
    cos, sin = _rope_tables(positions)
    h = x
    for i in range(depth):
        m, j = i % N_MIXERS, i // N_MIXERS
        if m == 0:
            h = _pool_mixer(h, mix_norm[i], pool_w[j], pool_scale[j])
        elif m == 1:
            qkv = _qkv_proj(h, mix_norm[i], diff_w_qkv[j].astype(BF16), 0, 3 * dm, cos, sin, 1)
            lam_params = jnp.zeros((SUBLANES, LANES), F32)
            lam_params = lam_params.at[0:4, 0:HEAD_DIM].set(
                jnp.stack([diff_lam_q1[j], diff_lam_k1[j], diff_lam_q2[j], diff_lam_k2[j]]).astype(F32))
            lam_init = 0.8 - 0.6 * math.exp(-0.3 * i)
            o = _diff_attention(qkv, lam_params, diff_subln[j], lam_init)
            h = _out_proj_hm(o, diff_w_o[j].astype(BF16), h)
        elif m == 2:
            gu = _norm_proj(h.reshape(b * s, dm), mix_norm[i], lru_w_in[j].astype(BF16)).reshape(b, s, -1)
            y = _lru_core(gu, lru_conv_w[j], lru_conv_b[j], lru_w_a[j], lru_b_a[j], lru_w_x[j], lru_b_x[j],
                          lru_lambda[j])
            h = _out_proj(y, lru_w_out[j].astype(BF16), h)
        else:
            w = dil_w_qkv[j].astype(BF16)
            groups = [_qkv_proj(h, mix_norm[i], w, g * 3 * dm, 3 * dm, cos, sin, d)
                      for g, (_, d) in enumerate(DIL_GROUPS)]
            o = _dilated_attention(groups)
            h = _out_proj_hm(o, dil_w_o[j].astype(BF16), h)
        h = _conv_ffn(h, ffn_norm[i], ffn_w_up[i].astype(BF16), ffn_conv_w[i], ffn_conv_b[i],
                      ffn_w_down[i].astype(BF16), final_norm if i == depth - 1 else None)
    return h
```

```python
import functools
import math

import jax
import jax.numpy as jnp
from jax import lax
from jax.experimental import pallas as pl
from jax.experimental.pallas import tpu as pltpu

F32 = jnp.float32
BF16 = jnp.bfloat16

HEAD_DIM = 64
NORM_EPS = 1e-6
ROPE_THETA = 10000.0
NEG_INF = -1e30
POOL_WINDOWS = (2, 4, 8, 16)
DIFF_SUBLN_EPS = 1e-5
LRU_C = 8.0
LRU_CONV = 4
DIL_GROUPS = ((128, 1), (512, 4), (2048, 16))
FFN_CONV = 3
N_MIXERS = 4

LANES = 128
SUBLANES = 8
VMEM_LIMIT_BYTES = 56 * 1024 * 1024


def _params(*semantics):
    return pltpu.CompilerParams(dimension_semantics=semantics, vmem_limit_bytes=VMEM_LIMIT_BYTES)


def _rms_rows(x, gain, eps):
    ms = jnp.mean(x * x, axis=-1, keepdims=True)
    return x * lax.rsqrt(ms + eps) * gain


def _halo_window(ref, r0, rows, cols=slice(None)):
    return ref[pl.ds(r0 - SUBLANES, rows + 2 * SUBLANES), cols]


def _shift_rows(win, off):
    if off == 0:
        return win
    return pltpu.roll(win, (-off) % win.shape[0], 0)


def _rope_table_kernel(pos_ref, inv_ref, cos_ref, sin_ref):
    ang = pos_ref[...] * inv_ref[...]
    lane = lax.broadcasted_iota(jnp.int32, ang.shape, 1)
    first_half = (lane % HEAD_DIM) < (HEAD_DIM // 2)
    cos_ref[...] = jnp.cos(ang)
    sin_ref[...] = jnp.where(first_half, -jnp.sin(ang), jnp.sin(ang))


def _rope_tables(positions):
    s = positions.shape[0]
    inv = ROPE_THETA ** (-jnp.arange(0, HEAD_DIM, 2, dtype=F32) / HEAD_DIM)
    inv = jnp.tile(inv, LANES // (HEAD_DIM // 2)).reshape(1, LANES)
    pos = positions.astype(F32).reshape(s, 1)
    return pl.pallas_call(
        _rope_table_kernel,
        out_shape=(jax.ShapeDtypeStruct((s, LANES), F32), jax.ShapeDtypeStruct((s, LANES), F32)),
        name="rope_tables",
    )(pos, inv)


Q_SCALE = HEAD_DIM ** -0.5 * math.log2(math.e)
QKV_TM = 1024
QKV_TN = 512


def _qkv_proj_kernel(x_ref, g_ref, wq_ref, wk_ref, wv_ref, cos_ref, sin_ref, q_ref, k_ref, v_ref, xn_ref,
                     *scratch, d):
    j = pl.program_id(2)
    tm, dm = x_ref.shape[1], x_ref.shape[2]
    nt = tm // d

    def chain_rows(ref, r):
        return ref[pl.ds(r, nt, stride=d), :]

    @pl.when(j == 0)
    def _():
        xnorm = _rms_rows(x_ref[0], g_ref[...], NORM_EPS)
        if d == 1:
            xn_ref[...] = xnorm.astype(BF16)
        else:
            xs_ref = scratch[0]
            for c in range(dm // LANES):
                xs_ref[c] = xnorm[:, c * LANES:(c + 1) * LANES]
            for r in range(d):
                for c in range(dm // LANES):
                    xn_ref[r * nt:(r + 1) * nt, c * LANES:(c + 1) * LANES] = chain_rows(xs_ref.at[c], r).astype(BF16)

    if d == 1:
        ct, st = cos_ref[...], sin_ref[...]
    else:
        ct = jnp.concatenate([chain_rows(cos_ref, r) for r in range(d)], axis=0)
        st = jnp.concatenate([chain_rows(sin_ref, r) for r in range(d)], axis=0)
    lane = lax.broadcasted_iota(jnp.int32, (1, LANES), 1)
    first_half = (lane % HEAD_DIM) < (HEAD_DIM // 2)
    xn = xn_ref[...]
    for part, (w_ref, o_ref) in enumerate(((wq_ref, q_ref), (wk_ref, k_ref), (wv_ref, v_ref))):
        y = jnp.dot(xn, w_ref[...], preferred_element_type=F32)
        for c in range(w_ref.shape[1] // LANES):
            yc = y[:, c * LANES:(c + 1) * LANES]
            if part < 2:
                partner = jnp.where(first_half, pltpu.roll(yc, LANES - HEAD_DIM // 2, 1),
                                    pltpu.roll(yc, HEAD_DIM // 2, 1))
                yc = yc * ct + partner * st
            if part == 0:
                yc = yc * Q_SCALE
            for r in range(d):
                o_ref[0, c, r] = yc[r * nt:(r + 1) * nt, :].astype(BF16)


def _qkv_proj(h, gain, w, col0, cos, sin, dilation):
    b, s, dm = h.shape
    d = dilation
    n = s // d
    tm, tn = QKV_TM, QKV_TN
    nt = tm // d
    assert s % tm == 0 and dm % tn == 0 and col0 % tn == 0 and nt % 16 == 0
    c0 = col0 // tn
    per = dm // tn
    out = jax.ShapeDtypeStruct((b, dm // LANES, d, n, LANES), BF16)
    out_spec = pl.BlockSpec((1, tn // LANES, d, nt, LANES), lambda bi, i, j: (bi, j, 0, i, 0))
    scratch = [pltpu.VMEM((tm, dm), BF16)]
    if d > 1:
        scratch.append(pltpu.VMEM((dm // LANES, tm, LANES), F32))
    return pl.pallas_call(
        functools.partial(_qkv_proj_kernel, d=d),
        grid=(b, s // tm, per),
        in_specs=[
            pl.BlockSpec((1, tm, dm), lambda bi, i, j: (bi, i, 0)),
            pl.BlockSpec((1, dm), lambda bi, i, j: (0, 0)),
            pl.BlockSpec((dm, tn), lambda bi, i, j: (0, c0 + j)),
            pl.BlockSpec((dm, tn), lambda bi, i, j: (0, c0 + per + j)),
            pl.BlockSpec((dm, tn), lambda bi, i, j: (0, c0 + 2 * per + j)),
            pl.BlockSpec((tm, LANES), lambda bi, i, j: (i, 0)),
            pl.BlockSpec((tm, LANES), lambda bi, i, j: (i, 0)),
        ],
        out_specs=(out_spec, out_spec, out_spec),
        out_shape=(out, out, out),
        scratch_shapes=scratch,
        compiler_params=_params("parallel", "parallel", "arbitrary"),
        name=f"qkv_proj_d{d}",
    )(h, gain.reshape(1, dm), w, w, w, cos, sin)


def _norm_proj_kernel(x_ref, g_ref, w_ref, o_ref, xn_ref):
    @pl.when(pl.program_id(1) == 0)
    def _():
        xn_ref[...] = _rms_rows(x_ref[...], g_ref[...], NORM_EPS).astype(BF16)

    o_ref[...] = jnp.dot(xn_ref[...], w_ref[...], preferred_element_type=F32)


def _norm_proj(h2d, gain, w):
    m, dm = h2d.shape
    n = w.shape[1]
    tm, tn = 1024, 1024
    assert m % tm == 0 and n % tn == 0
    return pl.pallas_call(
        _norm_proj_kernel,
        grid=(m // tm, n // tn),
        in_specs=[
            pl.BlockSpec((tm, dm), lambda i, j: (i, 0)),
            pl.BlockSpec((1, dm), lambda i, j: (0, 0)),
            pl.BlockSpec((dm, tn), lambda i, j: (0, j)),
        ],
        out_specs=pl.BlockSpec((tm, tn), lambda i, j: (i, j)),
        out_shape=jax.ShapeDtypeStruct((m, n), F32),
        scratch_shapes=[pltpu.VMEM((tm, dm), BF16)],
        compiler_params=_params("parallel", "arbitrary"),
        name="norm_proj",
    )(h2d, gain.reshape(1, dm), w)


def _out_proj_hm_kernel(a_ref, w_ref, h_ref, o_ref):
    a = jnp.concatenate([a_ref[0, p] for p in range(a_ref.shape[1])], axis=-1)
    o_ref[0] = h_ref[0] + jnp.dot(a, w_ref[...], preferred_element_type=F32)


def _out_proj_hm(a_hm, w, h):
    b, s, dm = h.shape
    kb = a_hm.shape[1]
    tm = 1024
    return pl.pallas_call(
        _out_proj_hm_kernel,
        grid=(b, s // tm),
        in_specs=[
            pl.BlockSpec((1, kb, tm, LANES), lambda bi, i: (bi, 0, i, 0)),
            pl.BlockSpec(w.shape, lambda bi, i: (0, 0)),
            pl.BlockSpec((1, tm, dm), lambda bi, i: (bi, i, 0)),
        ],
        out_specs=pl.BlockSpec((1, tm, dm), lambda bi, i: (bi, i, 0)),
        out_shape=jax.ShapeDtypeStruct(h.shape, F32),
        compiler_params=_params("parallel", "parallel"),
        name="out_proj_hm",
    )(a_hm, w, h)


def _out_proj_kernel(a_ref, w_ref, h_ref, o_ref):
    o_ref[0] = h_ref[0] + jnp.dot(a_ref[0], w_ref[...], preferred_element_type=F32)


def _out_proj(a, w, h):
    b, s, dm = h.shape
    k = a.shape[-1]
    tm = 1024
    return pl.pallas_call(
        _out_proj_kernel,
        grid=(b, s // tm),
        in_specs=[
            pl.BlockSpec((1, tm, k), lambda bi, i: (bi, i, 0)),
            pl.BlockSpec(w.shape, lambda bi, i: (0, 0)),
            pl.BlockSpec((1, tm, dm), lambda bi, i: (bi, i, 0)),
        ],
        out_specs=pl.BlockSpec((1, tm, dm), lambda bi, i: (bi, i, 0)),
        out_shape=jax.ShapeDtypeStruct(h.shape, F32),
        compiler_params=_params("parallel", "parallel"),
        name="out_proj",
    )(a, w, h)


POOL_PAD = 8
POOL_ROWS = 256


def _pool_kernel(h_ref, g_ref, w_ref, sc_ref, o_ref, hn_ref):
    s, dm = h_ref.shape[1], h_ref.shape[2]
    group = dm // len(POOL_WINDOWS)
    zeros = jnp.zeros((POOL_PAD, dm), F32)
    hn_ref[0:POOL_PAD, :] = zeros
    hn_ref[POOL_PAD + s:POOL_PAD + s + POOL_PAD, :] = zeros

    def norm_step(i, carry):
        r0 = pl.multiple_of(i * POOL_ROWS, POOL_ROWS)
        hn_ref[pl.ds(POOL_PAD + r0, POOL_ROWS), :] = _rms_rows(h_ref[0, pl.ds(r0, POOL_ROWS), :], g_ref[...], NORM_EPS)
        return carry

    lax.fori_loop(0, s // POOL_ROWS, norm_step, 0)

    def mix_step(i, carry):
        r0 = pl.multiple_of(i * POOL_ROWS, POOL_ROWS)
        t = r0 + lax.broadcasted_iota(jnp.int32, (POOL_ROWS, 1), 0)
        inner = slice(SUBLANES, SUBLANES + POOL_ROWS)
        for g, win in enumerate(POOL_WINDOWS):
            half = win // 2
            sl = slice(g * group, (g + 1) * group)
            xw = _halo_window(hn_ref, POOL_PAD + r0, POOL_ROWS, sl)
            tot = xw + _shift_rows(xw, -1)
            step = 1
            while 2 * step < win:
                tot = _shift_rows(tot, -step) + _shift_rows(tot, step)
                step *= 2
            cnt = (jnp.minimum(t + half, s) - jnp.maximum(t - half, 0)).astype(F32)
            pooled = (tot[inner] / cnt - xw[inner]).astype(BF16)
            y = jnp.dot(pooled, w_ref[g], preferred_element_type=F32) * sc_ref[:, sl]
            o_ref[0, pl.ds(r0, POOL_ROWS), sl] = h_ref[0, pl.ds(r0, POOL_ROWS), sl] + y
        return carry

    lax.fori_loop(0, s // POOL_ROWS, mix_step, 0)


def _pool_mixer(h, gain, w, scale):
    b, s, dm = h.shape
    return pl.pallas_call(
        _pool_kernel,
        grid=(b,),
        in_specs=[
            pl.BlockSpec((1, s, dm), lambda bi: (bi, 0, 0)),
            pl.BlockSpec((1, dm), lambda bi: (0, 0)),
            pl.BlockSpec(w.shape, lambda bi: (0, 0, 0)),
            pl.BlockSpec((1, dm), lambda bi: (0, 0)),
        ],
        out_specs=pl.BlockSpec((1, s, dm), lambda bi: (bi, 0, 0)),
        out_shape=jax.ShapeDtypeStruct(h.shape, F32),
        scratch_shapes=[pltpu.VMEM((s + 2 * POOL_PAD, dm), F32)],
        compiler_params=_params("parallel"),
        name="pool_mixer",
    )(h, gain.reshape(1, dm), w.astype(BF16), scale.reshape(1, dm))


DIFF_TQ = 256
DIFF_CHAINS = 2


def _diff_attn_kernel(q_ref, k_ref, v_ref, lam_ref, sub_ref, o_ref, *, lam_init):
    k = k_ref[0, 0, 0]
    v = v_ref[0, 0, 0]
    lp = lam_ref[...]
    lam = (jnp.exp(jnp.sum(lp[0:1] * lp[1:2], axis=-1, keepdims=True))
           - jnp.exp(jnp.sum(lp[2:3] * lp[3:4], axis=-1, keepdims=True)) + lam_init)
    lane = lax.broadcasted_iota(jnp.int32, (1, LANES), 1)
    first = lane < HEAD_DIM

    def softmax_parts(qm):
        sc = lax.dot_general(qm, k, (((1,), (1,)), ((), ())), preferred_element_type=F32)
        p = jnp.exp2(sc - jnp.max(sc, axis=-1, keepdims=True))
        return p.astype(BF16), jnp.sum(p, axis=-1, keepdims=True)

    for c in range(DIFF_CHAINS):
        rows = slice(c * DIFF_TQ, (c + 1) * DIFF_TQ)
        q = q_ref[0, 0, 0, rows, :]
        zero = jnp.zeros_like(q)
        p0, l0 = softmax_parts(jnp.where(first, q, zero))
        p1, l1 = softmax_parts(jnp.where(first, zero, q))
        a = p0 * (1.0 / l0).astype(BF16) - p1 * (lam / l1).astype(BF16)
        o = jnp.dot(a, v, preferred_element_type=F32)
        o = _rms_rows(o, sub_ref[...], DIFF_SUBLN_EPS) * (1.0 - lam_init)
        o_ref[0, 0, rows, :] = o.astype(BF16)


def _diff_attention(q, k, v, lam_params, subln, lam_init):
    b, heads, _, s, _ = q.shape
    tq = DIFF_TQ * DIFF_CHAINS
    return pl.pallas_call(
        functools.partial(_diff_attn_kernel, lam_init=lam_init),
        grid=(b, heads, s // tq),
        in_specs=[
            pl.BlockSpec((1, 1, 1, tq, LANES), lambda bi, h, i: (bi, h, 0, i, 0)),
            pl.BlockSpec((1, 1, 1, s, LANES), lambda bi, h, i: (bi, h, 0, 0, 0)),
            pl.BlockSpec((1, 1, 1, s, LANES), lambda bi, h, i: (bi, h, 0, 0, 0)),
            pl.BlockSpec(lam_params.shape, lambda bi, h, i: (0, 0)),
            pl.BlockSpec((1, LANES), lambda bi, h, i: (0, 0)),
        ],
        out_specs=pl.BlockSpec((1, 1, tq, LANES), lambda bi, h, i: (bi, h, i, 0)),
        out_shape=jax.ShapeDtypeStruct((b, heads, s, LANES), BF16),
        compiler_params=_params("parallel", "parallel", "parallel"),
        name="diff_attention",
    )(q, k, v, lam_params, subln.reshape(1, LANES))


LRU_PAD = 8
LRU_ROWS = 256


def _log_sigmoid(x):
    return jnp.minimum(x, 0.0) - jnp.log1p(jnp.exp(-jnp.abs(x)))


def _gelu_tanh(x):
    return 0.5 * x * (1.0 + jnp.tanh(math.sqrt(2.0 / math.pi) * (x + 0.044715 * (x * x * x))))


def _lru_kernel(gate_ref, u_ref, cw_ref, cb_ref, wbd_ref, bias_ref, lam_ref, o_ref,
                upad_ref, a_ref, b_ref, hs_ref):
    s, cw = u_ref.shape[1], u_ref.shape[2]
    zeros = jnp.zeros((LRU_PAD, cw), F32)
    upad_ref[0:LRU_PAD, :] = zeros
    upad_ref[LRU_PAD + s:LRU_PAD + s + LRU_PAD, :] = zeros

    def copy_step(i, carry):
        r0 = pl.multiple_of(i * LRU_ROWS, LRU_ROWS)
        upad_ref[pl.ds(LRU_PAD + r0, LRU_ROWS), :] = u_ref[0, pl.ds(r0, LRU_ROWS), :]
        return carry

    lax.fori_loop(0, s // LRU_ROWS, copy_step, 0)
    row = lax.broadcasted_iota(jnp.int32, (LRU_ROWS, 1), 0) % SUBLANES

    for dr in range(2):
        log_a_unit = LRU_C * _log_sigmoid(lam_ref[dr:dr + 1, :])

        def gate_step(i, carry, dr=dr, log_a_unit=log_a_unit):
            r0 = pl.multiple_of(i * LRU_ROWS, LRU_ROWS)
            uw = _halo_window(upad_ref, LRU_PAD + r0, LRU_ROWS)
            xc = cb_ref[dr:dr + 1, :]
            for j in range(LRU_CONV):
                off = (j - (LRU_CONV - 1)) if dr == 0 else ((LRU_CONV - 1) - j)
                xc = xc + cw_ref[dr, j:j + 1, :] * _shift_rows(uw, off)[SUBLANES:SUBLANES + LRU_ROWS]
            pre = jnp.dot(xc.astype(BF16), wbd_ref[dr, 0], preferred_element_type=F32) + bias_ref[0, dr:dr + 1, :]
            r = jax.nn.sigmoid(pre[:, :cw])
            ig = jax.nn.sigmoid(pre[:, cw:])
            log_a = r * log_a_unit
            a = jnp.exp(log_a)
            th = jnp.tanh(log_a)
            bv = jnp.sqrt(-2.0 * th / (1.0 - th)) * (ig * xc)
            for k in (1, 2, 4):
                off = -k if dr == 0 else k
                keep = (row >= k) if dr == 0 else (row < SUBLANES - k)
                a_sh = _shift_rows(a, off)
                b_sh = _shift_rows(bv, off)
                bv = jnp.where(keep, a * b_sh + bv, bv)
                a = jnp.where(keep, a * a_sh, a)
            a_ref[pl.ds(r0, LRU_ROWS), :] = a
            b_ref[pl.ds(r0, LRU_ROWS), :] = bv
            return carry

        lax.fori_loop(0, s // LRU_ROWS, gate_step, 0)
        n_tiles = s // SUBLANES

        def scan_step(i, carry, dr=dr):
            g = i if dr == 0 else n_tiles - 1 - i
            r0 = pl.multiple_of(g * SUBLANES, SUBLANES)
            hcur = b_ref[pl.ds(r0, SUBLANES), :] + a_ref[pl.ds(r0, SUBLANES), :] * carry
            if dr == 0:
                hs_ref[pl.ds(r0, SUBLANES), :] = hcur
                return hcur[SUBLANES - 1:SUBLANES, :]
            hs_ref[pl.ds(r0, SUBLANES), :] = hs_ref[pl.ds(r0, SUBLANES), :] + hcur
            return hcur[0:1, :]

        lax.fori_loop(0, n_tiles, scan_step, jnp.zeros((1, cw), F32), unroll=8)

    def out_step(i, carry):
        r0 = pl.multiple_of(i * LRU_ROWS, LRU_ROWS)
        y = _gelu_tanh(gate_ref[0, pl.ds(r0, LRU_ROWS), :]) * hs_ref[pl.ds(r0, LRU_ROWS), :]
        o_ref[0, pl.ds(r0, LRU_ROWS), :] = y.astype(BF16)
        return carry

    lax.fori_loop(0, s // LRU_ROWS, out_step, 0)


def _lru_core(gu, conv_w, conv_b, w_a, b_a, w_x, b_x, lam):
    b, s, c2 = gu.shape
    c = c2 // 2
    cw = 256
    ncb = c // cw
    per = cw // HEAD_DIM
    eye = jnp.eye(per, dtype=F32)

    def block_diag(w):
        w5 = w.reshape(2, ncb, per, HEAD_DIM, HEAD_DIM)
        return jnp.einsum('dcipq,ik->dcipkq', w5, eye).reshape(2, ncb, cw, cw)

    wbd = jnp.concatenate([block_diag(w_a), block_diag(w_x)], axis=-1).astype(BF16)
    bias = jnp.concatenate([b_a.reshape(2, ncb, cw), b_x.reshape(2, ncb, cw)], axis=-1)
    bias = bias.transpose(1, 0, 2)
    return pl.pallas_call(
        _lru_kernel,
        grid=(b, ncb),
        in_specs=[
            pl.BlockSpec((1, s, cw), lambda bi, cb: (bi, 0, cb)),
            pl.BlockSpec((1, s, cw), lambda bi, cb: (bi, 0, ncb + cb)),
            pl.BlockSpec((2, LRU_CONV, cw), lambda bi, cb: (0, 0, cb)),
            pl.BlockSpec((2, cw), lambda bi, cb: (0, cb)),
            pl.BlockSpec((2, 1, cw, 2 * cw), lambda bi, cb: (0, cb, 0, 0)),
            pl.BlockSpec((1, 2, 2 * cw), lambda bi, cb: (cb, 0, 0)),
            pl.BlockSpec((2, cw), lambda bi, cb: (0, cb)),
        ],
        out_specs=pl.BlockSpec((1, s, cw), lambda bi, cb: (bi, 0, cb)),
        out_shape=jax.ShapeDtypeStruct((b, s, c), BF16),
        scratch_shapes=[
            pltpu.VMEM((s + 2 * LRU_PAD, cw), F32),
            pltpu.VMEM((s, cw), F32),
            pltpu.VMEM((s, cw), F32),
            pltpu.VMEM((s, cw), F32),
        ],
        compiler_params=_params("parallel", "parallel"),
        name="lru_core",
    )(gu, gu, conv_w, conv_b, wbd, bias, lam)


DIL_TQ = 128
DIL_UNROLL = 4


def _band_bias(tq, w, rel0, half):
    qi = lax.broadcasted_iota(jnp.int32, (tq, 1), 0)
    ki = lax.broadcasted_iota(jnp.int32, (1, w), 1)
    return jnp.where(jnp.abs(ki + rel0 - qi) <= half, 0.0, NEG_INF).astype(F32)


def _band_tile(q, k, v, bias):
    tq = q.shape[0]
    lane = lax.broadcasted_iota(jnp.int32, (1, LANES), 1)
    first = lane < HEAD_DIM
    zero = jnp.zeros_like(q)
    qq = jnp.concatenate([jnp.where(first, q, zero), jnp.where(first, zero, q)], axis=0)
    sc = lax.dot_general(qq, k, (((1,), (1,)), ((), ())), preferred_element_type=F32)
    sc = sc + jnp.concatenate([bias, bias], axis=0)
    m = jnp.max(sc, axis=-1, keepdims=True)
    p = jnp.exp2(sc - m).astype(BF16)
    v1 = jnp.concatenate([v, jnp.ones_like(v)], axis=-1)
    pv = jnp.dot(p, v1, preferred_element_type=F32)
    m_b = jnp.broadcast_to(m, (2 * tq, LANES))
    l_b = jnp.broadcast_to(pv[:, LANES:LANES + 1], (2 * tq, LANES))
    pick = lambda t: jnp.where(first, t[:tq], t[tq:])
    return pick(m_b), pick(l_b), pick(pv[:, :LANES])


def _dilated_attn_kernel(q0_ref, k0_ref, v0_ref, q1_ref, k1_ref, v1_ref, q2_ref, k2_ref, v2_ref,
                         o_ref, m_ref, l_ref, acc_ref, bias_ref, *, halves, dils):
    s = o_ref.shape[2]
    tq = DIL_TQ
    win = 2 * tq

    def merge(rows, m_new, l_new, acc_new):
        m_old, l_old, acc_old = m_ref[rows, :], l_ref[rows, :], acc_ref[rows, :]
        m_tot = jnp.maximum(m_old, m_new)
        alpha = jnp.exp2(m_old - m_tot)
        beta = jnp.exp2(m_new - m_tot)
        m_ref[rows, :] = m_tot
        l_ref[rows, :] = alpha * l_old + beta * l_new
        acc_ref[rows, :] = alpha * acc_old + beta * acc_new

    groups = ((q0_ref, k0_ref, v0_ref), (q1_ref, k1_ref, v1_ref), (q2_ref, k2_ref, v2_ref))
    for g, (q_ref, k_ref, v_ref) in enumerate(groups):
        half, d = halves[g], dils[g]
        n = s // d
        tiles = n // tq
        w = min(win, n)
        for case, rel0 in enumerate((0, -(tq // 2), tq - w) if tiles > 1 else (0,)):
            bias_ref[case, :, 0:w] = _band_bias(tq, w, rel0, half)

        def tile_step(c, carry, g=g, q_ref=q_ref, k_ref=k_ref, v_ref=v_ref, d=d, n=n, tiles=tiles, w=w):
            r = c // tiles
            i = c % tiles
            q0 = pl.multiple_of(i * tq, tq)
            ks = pl.multiple_of(jnp.clip(q0 - tq // 2, 0, n - w), tq // 2)
            case = jnp.where(i == 0, 0, jnp.where(i == tiles - 1, 2, 1)) if tiles > 1 else 0
            m_new, l_new, acc_new = _band_tile(q_ref[0, 0, r, pl.ds(q0, tq), :],
                                               k_ref[0, 0, r, pl.ds(ks, w), :],
                                               v_ref[0, 0, r, pl.ds(ks, w), :],
                                               bias_ref[case, :, 0:w])
            if g == 0:
                m_ref[pl.ds(q0, tq), :] = m_new
                l_ref[pl.ds(q0, tq), :] = l_new
                acc_ref[pl.ds(q0, tq), :] = acc_new
            else:
                merge(pl.ds(q0 * d + r, tq, stride=d), m_new, l_new, acc_new)
            return carry

        lax.fori_loop(0, d * tiles, tile_step, 0, unroll=DIL_UNROLL)

    def out_step(i, carry):
        r0 = pl.multiple_of(i * 256, 256)
        o_ref[0, 0, pl.ds(r0, 256), :] = (acc_ref[pl.ds(r0, 256), :] / l_ref[pl.ds(r0, 256), :]).astype(BF16)
        return carry

    lax.fori_loop(0, s // 256, out_step, 0)


def _dilated_attention(qkv_groups):
    b, pairs, d0, s, _ = qkv_groups[0][0].shape
    assert d0 == 1
    halves = tuple(w // (2 * d) for (w, d) in DIL_GROUPS)
    dils = tuple(d for (_, d) in DIL_GROUPS)
    in_specs, args = [], []
    for q, k, v in qkv_groups:
        d, n = q.shape[2], q.shape[3]
        for arr in (q, k, v):
            in_specs.append(pl.BlockSpec((1, 1, d, n, LANES), lambda bi, p: (bi, p, 0, 0, 0)))
            args.append(arr)
    return pl.pallas_call(
        functools.partial(_dilated_attn_kernel, halves=halves, dils=dils),
        grid=(b, pairs),
        in_specs=in_specs,
        out_specs=pl.BlockSpec((1, 1, s, LANES), lambda bi, p: (bi, p, 0, 0)),
        out_shape=jax.ShapeDtypeStruct((b, pairs, s, LANES), BF16),
        scratch_shapes=[pltpu.VMEM((s, LANES), F32), pltpu.VMEM((s, LANES), F32), pltpu.VMEM((s, LANES), F32),
                        pltpu.VMEM((3, DIL_TQ, 2 * DIL_TQ), F32)],
        compiler_params=_params("parallel", "parallel"),
        name="dilated_attention",
    )(*args)


FFN_HALO = 16
FFN_UP_ROWS = 1024
FFN_DOWN_ROWS = 512
FFN_CHUNK = 256


def _ffn_kernel(h_ref, g_ref, wg_ref, wu_ref, cw_ref, cb_ref, wd_ref, hr_ref, fg_ref, o_ref, xn_ref, hid_ref,
                *, final_norm, n_up):
    j = pl.program_id(1)
    s = h_ref.shape[1]

    @pl.when(j == 0)
    def _():
        zeros = jnp.zeros((FFN_HALO, xn_ref.shape[1]), BF16)
        xn_ref[0:FFN_HALO, :] = zeros
        xn_ref[FFN_HALO + s:FFN_HALO + s + FFN_HALO, :] = zeros

        def norm_step(i, carry):
            r0 = pl.multiple_of(i * FFN_DOWN_ROWS, FFN_DOWN_ROWS)
            x = h_ref[0, pl.ds(r0, FFN_DOWN_ROWS), :]
            xn_ref[pl.ds(FFN_HALO + r0, FFN_DOWN_ROWS), :] = _rms_rows(x, g_ref[...], NORM_EPS).astype(BF16)
            return carry

        lax.fori_loop(0, s // FFN_DOWN_ROWS, norm_step, 0)

    @pl.when(j < n_up)
    def _():
        inner = slice(FFN_HALO, FFN_HALO + FFN_UP_ROWS)
        wgu = jnp.concatenate([wg_ref[...], wu_ref[...]], axis=-1)
        for rc in range(s // FFN_UP_ROWS):
            xs = xn_ref[rc * FFN_UP_ROWS:rc * FFN_UP_ROWS + FFN_UP_ROWS + 2 * FFN_HALO, :]
            gu = jnp.dot(xs, wgu, preferred_element_type=F32)
            g = gu[:, :FFN_CHUNK]
            gc = cb_ref[...]
            for t in range(FFN_CONV):
                gc = gc + cw_ref[t:t + 1, :] * _shift_rows(g, t - FFN_CONV // 2)[inner]
            act = 0.5 * gc * (1.0 + lax.erf(gc * (1.0 / math.sqrt(2.0))))
            hid_ref[j, rc * FFN_UP_ROWS:(rc + 1) * FFN_UP_ROWS, :] = (act * gu[inner, FFN_CHUNK:]).astype(BF16)

    @pl.when(j >= n_up)
    def _():
        r0 = pl.multiple_of((j - n_up) * FFN_DOWN_ROWS, FFN_DOWN_ROWS)
        hid = jnp.concatenate([hid_ref[c, pl.ds(r0, FFN_DOWN_ROWS), :] for c in range(n_up)], axis=-1)
        out = hr_ref[0] + jnp.dot(hid, wd_ref[...], preferred_element_type=F32)
        if final_norm:
            out = _rms_rows(out, fg_ref[...], NORM_EPS)
        o_ref[0] = out


def _conv_ffn(h, gain, w_up, conv_w, conv_b, w_down, final_gain):
    b, s, dm = h.shape
    dff = w_down.shape[0]
    n_up = dff // FFN_CHUNK
    n_down = s // FFN_DOWN_ROWS
    assert dff % FFN_CHUNK == 0 and s % FFN_UP_ROWS == 0 and s % FFN_DOWN_ROWS == 0
    final_norm = final_gain is not None
    fg = (final_gain if final_norm else gain).reshape(1, dm)
    up_idx = lambda j: jnp.minimum(j, n_up - 1)
    down_idx = lambda j: jnp.maximum(j - n_up, 0)
    return pl.pallas_call(
        functools.partial(_ffn_kernel, final_norm=final_norm, n_up=n_up),
        grid=(b, n_up + n_down),
        in_specs=[
            pl.BlockSpec((1, s, dm), lambda bi, j: (bi, 0, 0), pipeline_mode=pl.Buffered(1)),
            pl.BlockSpec((1, dm), lambda bi, j: (0, 0)),
            pl.BlockSpec((dm, FFN_CHUNK), lambda bi, j: (0, up_idx(j))),
            pl.BlockSpec((dm, FFN_CHUNK), lambda bi, j: (0, n_up + up_idx(j))),
            pl.BlockSpec((FFN_CONV, FFN_CHUNK), lambda bi, j: (0, up_idx(j))),
            pl.BlockSpec((1, FFN_CHUNK), lambda bi, j: (0, up_idx(j))),
            pl.BlockSpec((dff, dm), lambda bi, j: (0, 0), pipeline_mode=pl.Buffered(1)),
            pl.BlockSpec((1, FFN_DOWN_ROWS, dm), lambda bi, j: (bi, down_idx(j), 0)),
            pl.BlockSpec((1, dm), lambda bi, j: (0, 0)),
        ],
        out_specs=pl.BlockSpec((1, FFN_DOWN_ROWS, dm), lambda bi, j: (bi, down_idx(j), 0)),
        out_shape=jax.ShapeDtypeStruct(h.shape, F32),
        scratch_shapes=[pltpu.VMEM((s + 2 * FFN_HALO, dm), BF16), pltpu.VMEM((n_up, s, FFN_CHUNK), BF16)],
        compiler_params=_params("parallel", "arbitrary"),
        name="conv_ffn",
    )(h, gain.reshape(1, dm), w_up, w_up, conv_w, conv_b.reshape(1, dff), w_down, h, fg)


def kernel(x, positions, mix_norm, pool_w, pool_scale, diff_w_qkv, diff_lam_q1, diff_lam_k1, diff_lam_q2,
           diff_lam_k2, diff_subln, diff_w_o, lru_w_in, lru_conv_w, lru_conv_b, lru_w_a, lru_b_a, lru_w_x,
           lru_b_x, lru_lambda, lru_w_out, dil_w_qkv, dil_w_o, ffn_norm, ffn_w_up, ffn_conv_w, ffn_conv_b,
           ffn_w_down, final_norm):
    b, s, dm = x.shape
    depth = mix_norm.shape[0]
    cos, sin = _rope_tables(positions)
    h = x
    for i in range(depth):
        m, j = i % N_MIXERS, i // N_MIXERS
        if m == 0:
            h = _pool_mixer(h, mix_norm[i], pool_w[j], pool_scale[j])
        elif m == 1:
            q, k, v = _qkv_proj(h, mix_norm[i], diff_w_qkv[j].astype(BF16), 0, cos, sin, 1)
            lam_params = jnp.zeros((SUBLANES, LANES), F32)
            lam_params = lam_params.at[0:4, 0:HEAD_DIM].set(
                jnp.stack([diff_lam_q1[j], diff_lam_k1[j], diff_lam_q2[j], diff_lam_k2[j]]).astype(F32))
            lam_init = 0.8 - 0.6 * math.exp(-0.3 * i)
            o = _diff_attention(q, k, v, lam_params, diff_subln[j], lam_init)
            h = _out_proj_hm(o, diff_w_o[j].astype(BF16), h)
        elif m == 2:
            gu = _norm_proj(h.reshape(b * s, dm), mix_norm[i], lru_w_in[j].astype(BF16)).reshape(b, s, -1)
            y = _lru_core(gu, lru_conv_w[j], lru_conv_b[j], lru_w_a[j], lru_b_a[j], lru_w_x[j], lru_b_x[j],
                          lru_lambda[j])
            h = _out_proj(y, lru_w_out[j].astype(BF16), h)
        else:
            w = dil_w_qkv[j].astype(BF16)
            groups = [_qkv_proj(h, mix_norm[i], w, g * 3 * dm, cos, sin, d)
                      for g, (_, d) in enumerate(DIL_GROUPS)]
            o = _dilated_attention(groups)
            h = _out_proj_hm(o, dil_w_o[j].astype(BF16), h)
        h = _conv_ffn(h, ffn_norm[i], ffn_w_up[i].astype(BF16), ffn_conv_w[i], ffn_conv_b[i],
                      ffn_w_down[i].astype(BF16), final_norm if i == depth - 1 else None)
    return h
```

```python
import functools
import math

import jax
import jax.numpy as jnp
from jax import lax
from jax.experimental import pallas as pl
from jax.experimental.pallas import tpu as pltpu

F32 = jnp.float32
BF16 = jnp.bfloat16

HEAD_DIM = 64
NORM_EPS = 1e-6
ROPE_THETA = 10000.0
NEG_INF = -1e30
POOL_WINDOWS = (2, 4, 8, 16)
DIFF_SUBLN_EPS = 1e-5
LRU_C = 8.0
LRU_CONV = 4
DIL_GROUPS = ((128, 1), (512, 4), (2048, 16))
FFN_CONV = 3
N_MIXERS = 4

LANES = 128
SUBLANES = 8
VMEM_LIMIT_BYTES = 56 * 1024 * 1024


def _params(*semantics):
    return pltpu.CompilerParams(dimension_semantics=semantics, vmem_limit_bytes=VMEM_LIMIT_BYTES)


def _rms_rows(x, gain, eps):
    ms = jnp.mean(x * x, axis=-1, keepdims=True)
    return x * lax.rsqrt(ms + eps) * gain


def _halo_window(ref, r0, rows, cols=slice(None)):
    return ref[pl.ds(r0 - SUBLANES, rows + 2 * SUBLANES), cols]


def _shift_rows(win, off):
    if off == 0:
        return win
    return pltpu.roll(win, (-off) % win.shape[0], 0)


def _rope_table_kernel(pos_ref, inv_ref, cos_ref, sin_ref):
    ang = pos_ref[...] * inv_ref[...]
    lane = lax.broadcasted_iota(jnp.int32, ang.shape, 1)
    first_half = (lane % HEAD_DIM) < (HEAD_DIM // 2)
    cos_ref[...] = jnp.cos(ang)
    sin_ref[...] = jnp.where(first_half, -jnp.sin(ang), jnp.sin(ang))


def _rope_tables(positions):
    s = positions.shape[0]
    inv = ROPE_THETA ** (-jnp.arange(0, HEAD_DIM, 2, dtype=F32) / HEAD_DIM)
    inv = jnp.tile(inv, LANES // (HEAD_DIM // 2)).reshape(1, LANES)
    pos = positions.astype(F32).reshape(s, 1)
    return pl.pallas_call(
        _rope_table_kernel,
        out_shape=(jax.ShapeDtypeStruct((s, LANES), F32), jax.ShapeDtypeStruct((s, LANES), F32)),
        name="rope_tables",
    )(pos, inv)


Q_SCALE = HEAD_DIM ** -0.5 * math.log2(math.e)
QKV_TM = 1024
QKV_TN = 512


def _qkv_proj_kernel(x_ref, g_ref, wq_ref, wk_ref, wv_ref, cos_ref, sin_ref, q_ref, k_ref, v_ref, xn_ref,
                     *scratch, d):
    j = pl.program_id(2)
    tm, dm = x_ref.shape[1], x_ref.shape[2]
    nt = tm // d

    def chain_rows(ref, r):
        return ref[pl.ds(r, nt, stride=d), :]

    @pl.when(j == 0)
    def _():
        xnorm = _rms_rows(x_ref[0], g_ref[...], NORM_EPS)
        if d == 1:
            xn_ref[...] = xnorm.astype(BF16)
        else:
            xs_ref = scratch[0]
            for c in range(dm // LANES):
                xs_ref[c] = xnorm[:, c * LANES:(c + 1) * LANES]
            for r in range(d):
                for c in range(dm // LANES):
                    xn_ref[r * nt:(r + 1) * nt, c * LANES:(c + 1) * LANES] = chain_rows(xs_ref.at[c], r).astype(BF16)

    if d == 1:
        ct, st = cos_ref[...], sin_ref[...]
    else:
        ct = jnp.concatenate([chain_rows(cos_ref, r) for r in range(d)], axis=0)
        st = jnp.concatenate([chain_rows(sin_ref, r) for r in range(d)], axis=0)
    lane = lax.broadcasted_iota(jnp.int32, (1, LANES), 1)
    first_half = (lane % HEAD_DIM) < (HEAD_DIM // 2)
    xn = xn_ref[...]
    for part, (w_ref, o_ref) in enumerate(((wq_ref, q_ref), (wk_ref, k_ref), (wv_ref, v_ref))):
        y = jnp.dot(xn, w_ref[...], preferred_element_type=F32)
        for c in range(w_ref.shape[1] // LANES):
            yc = y[:, c * LANES:(c + 1) * LANES]
            if part < 2:
                partner = jnp.where(first_half, pltpu.roll(yc, LANES - HEAD_DIM // 2, 1),
                                    pltpu.roll(yc, HEAD_DIM // 2, 1))
                yc = yc * ct + partner * st
            if part == 0:
                yc = yc * Q_SCALE
            for r in range(d):
                o_ref[0, c, r] = yc[r * nt:(r + 1) * nt, :].astype(BF16)


def _qkv_proj(h, gain, w, col0, cos, sin, dilation):
    b, s, dm = h.shape
    d = dilation
    n = s // d
    tm, tn = QKV_TM, QKV_TN
    nt = tm // d
    assert s % tm == 0 and dm % tn == 0 and col0 % tn == 0 and nt % 16 == 0
    c0 = col0 // tn
    per = dm // tn
    out = jax.ShapeDtypeStruct((b, dm // LANES, d, n, LANES), BF16)
    out_spec = pl.BlockSpec((1, tn // LANES, d, nt, LANES), lambda bi, i, j: (bi, j, 0, i, 0))
    scratch = [pltpu.VMEM((tm, dm), BF16)]
    if d > 1:
        scratch.append(pltpu.VMEM((dm // LANES, tm, LANES), F32))
    return pl.pallas_call(
        functools.partial(_qkv_proj_kernel, d=d),
        grid=(b, s // tm, per),
        in_specs=[
            pl.BlockSpec((1, tm, dm), lambda bi, i, j: (bi, i, 0)),
            pl.BlockSpec((1, dm), lambda bi, i, j: (0, 0)),
            pl.BlockSpec((dm, tn), lambda bi, i, j: (0, c0 + j)),
            pl.BlockSpec((dm, tn), lambda bi, i, j: (0, c0 + per + j)),
            pl.BlockSpec((dm, tn), lambda bi, i, j: (0, c0 + 2 * per + j)),
            pl.BlockSpec((tm, LANES), lambda bi, i, j: (i, 0)),
            pl.BlockSpec((tm, LANES), lambda bi, i, j: (i, 0)),
        ],
        out_specs=(out_spec, out_spec, out_spec),
        out_shape=(out, out, out),
        scratch_shapes=scratch,
        compiler_params=_params("parallel", "parallel", "arbitrary"),
        name=f"qkv_proj_d{d}",
    )(h, gain.reshape(1, dm), w, w, w, cos, sin)


def _out_proj_hm_kernel(a_ref, w_ref, h_ref, o_ref):
    a = jnp.concatenate([a_ref[0, p] for p in range(a_ref.shape[1])], axis=-1)
    o_ref[0] = h_ref[0] + jnp.dot(a, w_ref[...], preferred_element_type=F32)


def _out_proj_hm(a_hm, w, h):
    b, s, dm = h.shape
    kb = a_hm.shape[1]
    tm = 1024
    return pl.pallas_call(
        _out_proj_hm_kernel,
        grid=(b, s // tm),
        in_specs=[
            pl.BlockSpec((1, kb, tm, LANES), lambda bi, i: (bi, 0, i, 0)),
            pl.BlockSpec(w.shape, lambda bi, i: (0, 0)),
            pl.BlockSpec((1, tm, dm), lambda bi, i: (bi, i, 0)),
        ],
        out_specs=pl.BlockSpec((1, tm, dm), lambda bi, i: (bi, i, 0)),
        out_shape=jax.ShapeDtypeStruct(h.shape, F32),
        compiler_params=_params("parallel", "parallel"),
        name="out_proj_hm",
    )(a_hm, w, h)


def _out_proj_kernel(a_ref, w_ref, h_ref, o_ref):
    o_ref[0] = h_ref[0] + jnp.dot(a_ref[0], w_ref[...], preferred_element_type=F32)


def _out_proj(a, w, h):
    b, s, dm = h.shape
    k = a.shape[-1]
    tm = 1024
    return pl.pallas_call(
        _out_proj_kernel,
        grid=(b, s // tm),
        in_specs=[
            pl.BlockSpec((1, tm, k), lambda bi, i: (bi, i, 0)),
            pl.BlockSpec(w.shape, lambda bi, i: (0, 0)),
            pl.BlockSpec((1, tm, dm), lambda bi, i: (bi, i, 0)),
        ],
        out_specs=pl.BlockSpec((1, tm, dm), lambda bi, i: (bi, i, 0)),
        out_shape=jax.ShapeDtypeStruct(h.shape, F32),
        compiler_params=_params("parallel", "parallel"),
        name="out_proj",
    )(a, w, h)


POOL_PAD = 8
POOL_ROWS = 256


def _pool_kernel(h_ref, g_ref, w_ref, sc_ref, o_ref, hn_ref):
    s, dm = h_ref.shape[1], h_ref.shape[2]
    group = dm // len(POOL_WINDOWS)
    zeros = jnp.zeros((POOL_PAD, dm), F32)
    hn_ref[0:POOL_PAD, :] = zeros
    hn_ref[POOL_PAD + s:POOL_PAD + s + POOL_PAD, :] = zeros

    def norm_step(i, carry):
        r0 = pl.multiple_of(i * POOL_ROWS, POOL_ROWS)
        hn_ref[pl.ds(POOL_PAD + r0, POOL_ROWS), :] = _rms_rows(h_ref[0, pl.ds(r0, POOL_ROWS), :], g_ref[...], NORM_EPS)
        return carry

    lax.fori_loop(0, s // POOL_ROWS, norm_step, 0)

    def mix_step(i, carry):
        r0 = pl.multiple_of(i * POOL_ROWS, POOL_ROWS)
        t = r0 + lax.broadcasted_iota(jnp.int32, (POOL_ROWS, 1), 0)
        inner = slice(SUBLANES, SUBLANES + POOL_ROWS)
        for g, win in enumerate(POOL_WINDOWS):
            half = win // 2
            sl = slice(g * group, (g + 1) * group)
            xw = _halo_window(hn_ref, POOL_PAD + r0, POOL_ROWS, sl)
            tot = xw + _shift_rows(xw, -1)
            step = 1
            while 2 * step < win:
                tot = _shift_rows(tot, -step) + _shift_rows(tot, step)
                step *= 2
            cnt = (jnp.minimum(t + half, s) - jnp.maximum(t - half, 0)).astype(F32)
            pooled = (tot[inner] / cnt - xw[inner]).astype(BF16)
            y = jnp.dot(pooled, w_ref[g], preferred_element_type=F32) * sc_ref[:, sl]
            o_ref[0, pl.ds(r0, POOL_ROWS), sl] = h_ref[0, pl.ds(r0, POOL_ROWS), sl] + y
        return carry

    lax.fori_loop(0, s // POOL_ROWS, mix_step, 0)


def _pool_mixer(h, gain, w, scale):
    b, s, dm = h.shape
    return pl.pallas_call(
        _pool_kernel,
        grid=(b,),
        in_specs=[
            pl.BlockSpec((1, s, dm), lambda bi: (bi, 0, 0)),
            pl.BlockSpec((1, dm), lambda bi: (0, 0)),
            pl.BlockSpec(w.shape, lambda bi: (0, 0, 0)),
            pl.BlockSpec((1, dm), lambda bi: (0, 0)),
        ],
        out_specs=pl.BlockSpec((1, s, dm), lambda bi: (bi, 0, 0)),
        out_shape=jax.ShapeDtypeStruct(h.shape, F32),
        scratch_shapes=[pltpu.VMEM((s + 2 * POOL_PAD, dm), F32)],
        compiler_params=_params("parallel"),
        name="pool_mixer",
    )(h, gain.reshape(1, dm), w.astype(BF16), scale.reshape(1, dm))


DIFF_TQ = 256
DIFF_CHAINS = 4
DIFF_AHEAD = 2


def _diff_attn_kernel(q_ref, k_ref, v_ref, lam_ref, sub_ref, o_ref, *, lam_init):
    k = k_ref[0, 0, 0]
    v = v_ref[0, 0, 0]
    lp = lam_ref[...]
    lam = (jnp.exp(jnp.sum(lp[0:1] * lp[1:2], axis=-1, keepdims=True))
           - jnp.exp(jnp.sum(lp[2:3] * lp[3:4], axis=-1, keepdims=True)) + lam_init)
    lane = lax.broadcasted_iota(jnp.int32, (1, LANES), 1)
    first = lane < HEAD_DIM

    def scores(c):
        q = q_ref[0, 0, 0, c * DIFF_TQ:(c + 1) * DIFF_TQ, :]
        zero = jnp.zeros_like(q)
        dn = (((1,), (1,)), ((), ()))
        return (lax.dot_general(jnp.where(first, q, zero), k, dn, preferred_element_type=F32),
                lax.dot_general(jnp.where(first, zero, q), k, dn, preferred_element_type=F32))

    def softmax_parts(sc):
        p = jnp.exp2(sc - jnp.max(sc, axis=-1, keepdims=True))
        return p.astype(BF16), jnp.sum(p, axis=-1, keepdims=True)

    pending = [scores(c) for c in range(min(DIFF_AHEAD, DIFF_CHAINS))]
    for c in range(DIFF_CHAINS):
        if c + DIFF_AHEAD < DIFF_CHAINS:
            pending.append(scores(c + DIFF_AHEAD))
        sc = pending.pop(0)
        p0, l0 = softmax_parts(sc[0])
        p1, l1 = softmax_parts(sc[1])
        a = p0 * (1.0 / l0).astype(BF16) - p1 * (lam / l1).astype(BF16)
        o = jnp.dot(a, v, preferred_element_type=F32)
        o = _rms_rows(o, sub_ref[...], DIFF_SUBLN_EPS) * (1.0 - lam_init)
        o_ref[0, 0, c * DIFF_TQ:(c + 1) * DIFF_TQ, :] = o.astype(BF16)


def _diff_attention(q, k, v, lam_params, subln, lam_init):
    b, heads, _, s, _ = q.shape
    tq = DIFF_TQ * DIFF_CHAINS
    return pl.pallas_call(
        functools.partial(_diff_attn_kernel, lam_init=lam_init),
        grid=(b, heads, s // tq),
        in_specs=[
            pl.BlockSpec((1, 1, 1, tq, LANES), lambda bi, h, i: (bi, h, 0, i, 0)),
            pl.BlockSpec((1, 1, 1, s, LANES), lambda bi, h, i: (bi, h, 0, 0, 0)),
            pl.BlockSpec((1, 1, 1, s, LANES), lambda bi, h, i: (bi, h, 0, 0, 0)),
            pl.BlockSpec(lam_params.shape, lambda bi, h, i: (0, 0)),
            pl.BlockSpec((1, LANES), lambda bi, h, i: (0, 0)),
        ],
        out_specs=pl.BlockSpec((1, 1, tq, LANES), lambda bi, h, i: (bi, h, i, 0)),
        out_shape=jax.ShapeDtypeStruct((b, heads, s, LANES), BF16),
        compiler_params=_params("parallel", "parallel", "parallel"),
        name="diff_attention",
    )(q, k, v, lam_params, subln.reshape(1, LANES))


LRU_PAD = 8
LRU_ROWS = 256
LRU_PROJ_ROWS = 1024
LRU_CW = 256


def _log_sigmoid(x):
    return jnp.minimum(x, 0.0) - jnp.log1p(jnp.exp(-jnp.abs(x)))


def _sigmoid(x):
    return 0.5 * jnp.tanh(0.5 * x) + 0.5


def _gelu_tanh(x):
    return 0.5 * x * (1.0 + jnp.tanh(math.sqrt(2.0 / math.pi) * (x + 0.044715 * (x * x * x))))


def _lru_kernel(h_ref, g_ref, wg_ref, wu_ref, cw_ref, cb_ref, wbd_ref, bias_ref, lam_ref, o_ref,
                xn_ref, gate_ref, upad_ref, a_ref, b_ref, c_ref):
    s = h_ref.shape[1]
    cw = wg_ref.shape[1]
    n_slab = cw // LANES
    assert s == SUBLANES * LRU_ROWS

    @pl.when(pl.program_id(1) == 0)
    def _():
        def norm_step(i, carry):
            r0 = pl.multiple_of(i * LRU_ROWS, LRU_ROWS)
            xn_ref[pl.ds(r0, LRU_ROWS), :] = _rms_rows(h_ref[0, pl.ds(r0, LRU_ROWS), :], g_ref[...],
                                                       NORM_EPS).astype(BF16)
            return carry

        lax.fori_loop(0, s // LRU_ROWS, norm_step, 0)

    zeros = jnp.zeros((LRU_PAD, cw), F32)
    upad_ref[0:LRU_PAD, :] = zeros
    upad_ref[LRU_PAD + s:LRU_PAD + s + LRU_PAD, :] = zeros
    wgu = jnp.concatenate([wg_ref[...], wu_ref[...]], axis=-1)
    for blk in range(s // LRU_PROJ_ROWS):
        rows = slice(blk * LRU_PROJ_ROWS, (blk + 1) * LRU_PROJ_ROWS)
        gu = jnp.dot(xn_ref[rows, :], wgu, preferred_element_type=F32)
        gate_ref[rows, :] = gu[:, :cw]
        upad_ref[LRU_PAD + blk * LRU_PROJ_ROWS:LRU_PAD + (blk + 1) * LRU_PROJ_ROWS, :] = gu[:, cw:]

    row8 = lax.broadcasted_iota(jnp.int32, (SUBLANES, 1), 0)
    for dr in range(2):
        log_a_unit = LRU_C * _log_sigmoid(lam_ref[dr:dr + 1, :])

        def gate_step(i, carry, dr=dr, log_a_unit=log_a_unit):
            r0 = pl.multiple_of(i * LRU_ROWS, LRU_ROWS)
            uw = _halo_window(upad_ref, LRU_PAD + r0, LRU_ROWS)
            xc = cb_ref[dr:dr + 1, :]
            for j in range(LRU_CONV):
                off = (j - (LRU_CONV - 1)) if dr == 0 else ((LRU_CONV - 1) - j)
                xc = xc + cw_ref[dr, j:j + 1, :] * _shift_rows(uw, off)[SUBLANES:SUBLANES + LRU_ROWS]
            pre = jnp.dot(xc.astype(BF16), wbd_ref[dr, 0], preferred_element_type=F32) + bias_ref[0, dr:dr + 1, :]
            r = _sigmoid(pre[:, :cw])
            ig = _sigmoid(pre[:, cw:])
            log_a = r * log_a_unit
            a = jnp.exp(log_a)
            th = jnp.tanh(log_a)
            bv = jnp.sqrt(-2.0 * th / (1.0 - th)) * (ig * xc)
            for sl in range(n_slab):
                lanes = slice(sl * LANES, (sl + 1) * LANES)
                a_ref[dr, sl, pl.ds(i, LRU_ROWS, stride=SUBLANES), :] = a[:, lanes]
                b_ref[dr, sl, pl.ds(i, LRU_ROWS, stride=SUBLANES), :] = bv[:, lanes]
            return carry

        lax.fori_loop(0, SUBLANES, gate_step, 0)

        def scan_step(t, carry, dr=dr):
            g = t if dr == 0 else LRU_ROWS - 1 - t
            r0 = pl.multiple_of(g * SUBLANES, SUBLANES)
            hs, ps = carry
            new_h, new_p = [], []
            for sl in range(n_slab):
                a_t = a_ref[dr, sl, pl.ds(r0, SUBLANES), :]
                h_t = a_t * hs[sl] + b_ref[dr, sl, pl.ds(r0, SUBLANES), :]
                p_t = a_t * ps[sl]
                b_ref[dr, sl, pl.ds(r0, SUBLANES), :] = h_t
                a_ref[dr, sl, pl.ds(r0, SUBLANES), :] = p_t
                new_h.append(h_t)
                new_p.append(p_t)
            return tuple(new_h), tuple(new_p)

        init = (tuple(jnp.zeros((SUBLANES, LANES), F32) for _ in range(n_slab)),
                tuple(jnp.ones((SUBLANES, LANES), F32) for _ in range(n_slab)))
        h_end, p_end = lax.fori_loop(0, LRU_ROWS, scan_step, init, unroll=8)

        for sl in range(n_slab):
            c = jnp.zeros((SUBLANES, LANES), F32)
            for _ in range(SUBLANES - 1):
                nxt = h_end[sl] + p_end[sl] * c
                if dr == 0:
                    c = jnp.where(row8 == 0, 0.0, _shift_rows(nxt, -1))
                else:
                    c = jnp.where(row8 == SUBLANES - 1, 0.0, _shift_rows(nxt, 1))
            c_ref[dr, sl] = c

    def fix_step(t, carry):
        r0 = pl.multiple_of(t * LRU_ROWS, LRU_ROWS)
        for sl in range(n_slab):
            tot = None
            for dr in range(2):
                local = b_ref[dr, sl, pl.ds(r0, LRU_ROWS), :].reshape(-1, SUBLANES, LANES)
                decay = a_ref[dr, sl, pl.ds(r0, LRU_ROWS), :].reshape(-1, SUBLANES, LANES)
                fixed = local + decay * c_ref[dr, sl]
                tot = fixed if tot is None else tot + fixed
            b_ref[0, sl, pl.ds(r0, LRU_ROWS), :] = tot.reshape(LRU_ROWS, LANES)
        return carry

    lax.fori_loop(0, SUBLANES, fix_step, 0)

    for i in range(SUBLANES):
        rows = slice(i * LRU_ROWS, (i + 1) * LRU_ROWS)
        hsum = jnp.concatenate([b_ref[0, sl, pl.ds(i, LRU_ROWS, stride=SUBLANES), :] for sl in range(n_slab)],
                               axis=-1)
        o_ref[0, rows, :] = (_gelu_tanh(gate_ref[rows, :]) * hsum).astype(BF16)


def _lru_core(h, gain, w_in, conv_w, conv_b, w_a, b_a, w_x, b_x, lam):
    b, s, dm = h.shape
    c = w_in.shape[1] // 2
    cw = LRU_CW
    ncb = c // cw
    per = cw // HEAD_DIM
    eye = jnp.eye(per, dtype=F32)

    def block_diag(w):
        w5 = w.reshape(2, ncb, per, HEAD_DIM, HEAD_DIM)
        return jnp.einsum('dcipq,ik->dcipkq', w5, eye).reshape(2, ncb, cw, cw)

    wbd = jnp.concatenate([block_diag(w_a), block_diag(w_x)], axis=-1).astype(BF16)
    bias = jnp.concatenate([b_a.reshape(2, ncb, cw), b_x.reshape(2, ncb, cw)], axis=-1)
    bias = bias.transpose(1, 0, 2)
    n_slab = cw // LANES
    return pl.pallas_call(
        _lru_kernel,
        grid=(b, ncb),
        in_specs=[
            pl.BlockSpec((1, s, dm), lambda bi, cb: (bi, 0, 0), pipeline_mode=pl.Buffered(1)),
            pl.BlockSpec((1, dm), lambda bi, cb: (0, 0)),
            pl.BlockSpec((dm, cw), lambda bi, cb: (0, cb)),
            pl.BlockSpec((dm, cw), lambda bi, cb: (0, ncb + cb)),
            pl.BlockSpec((2, LRU_CONV, cw), lambda bi, cb: (0, 0, cb)),
            pl.BlockSpec((2, cw), lambda bi, cb: (0, cb)),
            pl.BlockSpec((2, 1, cw, 2 * cw), lambda bi, cb: (0, cb, 0, 0)),
            pl.BlockSpec((1, 2, 2 * cw), lambda bi, cb: (cb, 0, 0)),
            pl.BlockSpec((2, cw), lambda bi, cb: (0, cb)),
        ],
        out_specs=pl.BlockSpec((1, s, cw), lambda bi, cb: (bi, 0, cb)),
        out_shape=jax.ShapeDtypeStruct((b, s, c), BF16),
        scratch_shapes=[
            pltpu.VMEM((s, dm), BF16),
            pltpu.VMEM((s, cw), F32),
            pltpu.VMEM((s + 2 * LRU_PAD, cw), F32),
            pltpu.VMEM((2, n_slab, s, LANES), F32),
            pltpu.VMEM((2, n_slab, s, LANES), F32),
            pltpu.VMEM((2, n_slab, SUBLANES, LANES), F32),
        ],
        compiler_params=_params("parallel", "arbitrary"),
        name="lru_core",
    )(h, gain.reshape(1, dm), w_in, w_in, conv_w, conv_b, wbd, bias, lam)


DIL_TQ = 128
DIL_UNROLL = 4


def _band_bias(tq, w, rel0, half):
    qi = lax.broadcasted_iota(jnp.int32, (tq, 1), 0)
    ki = lax.broadcasted_iota(jnp.int32, (1, w), 1)
    return jnp.where(jnp.abs(ki + rel0 - qi) <= half, 0.0, NEG_INF).astype(F32)


def _band_scores(q, k):
    lane = lax.broadcasted_iota(jnp.int32, (1, LANES), 1)
    first = lane < HEAD_DIM
    zero = jnp.zeros_like(q)
    qq = jnp.concatenate([jnp.where(first, q, zero), jnp.where(first, zero, q)], axis=0)
    return lax.dot_general(qq, k, (((1,), (1,)), ((), ())), preferred_element_type=F32)


def _band_softmax_pv(sc, v, bias):
    tq = sc.shape[0] // 2
    lane = lax.broadcasted_iota(jnp.int32, (1, LANES), 1)
    first = lane < HEAD_DIM
    sc = sc + jnp.concatenate([bias, bias], axis=0)
    m = jnp.max(sc, axis=-1, keepdims=True)
    p = jnp.exp2(sc - m).astype(BF16)
    v1 = jnp.concatenate([v, jnp.ones_like(v)], axis=-1)
    pv = jnp.dot(p, v1, preferred_element_type=F32)
    m_b = jnp.broadcast_to(m, (2 * tq, LANES))
    l_b = jnp.broadcast_to(pv[:, LANES:LANES + 1], (2 * tq, LANES))
    pick = lambda t: jnp.where(first, t[:tq], t[tq:])
    return pick(m_b), pick(l_b), pick(pv[:, :LANES])


def _dilated_attn_kernel(q0_ref, k0_ref, v0_ref, q1_ref, k1_ref, v1_ref, q2_ref, k2_ref, v2_ref,
                         o_ref, m_ref, l_ref, acc_ref, bias_ref, *, halves, dils):
    s = o_ref.shape[2]
    tq = DIL_TQ
    win = 2 * tq

    def merge(rows, m_new, l_new, acc_new):
        m_old, l_old, acc_old = m_ref[rows, :], l_ref[rows, :], acc_ref[rows, :]
        m_tot = jnp.maximum(m_old, m_new)
        alpha = jnp.exp2(m_old - m_tot)
        beta = jnp.exp2(m_new - m_tot)
        m_ref[rows, :] = m_tot
        l_ref[rows, :] = alpha * l_old + beta * l_new
        acc_ref[rows, :] = alpha * acc_old + beta * acc_new

    groups = ((q0_ref, k0_ref, v0_ref), (q1_ref, k1_ref, v1_ref), (q2_ref, k2_ref, v2_ref))
    for g, (q_ref, k_ref, v_ref) in enumerate(groups):
        half, d = halves[g], dils[g]
        n = s // d
        tiles = n // tq
        w = min(win, n)
        for case, rel0 in enumerate((0, -(tq // 2), tq - w) if tiles > 1 else (0,)):
            bias_ref[case, :, 0:w] = _band_bias(tq, w, rel0, half)

        def batch_step(cb, carry, g=g, q_ref=q_ref, k_ref=k_ref, v_ref=v_ref, d=d, n=n, tiles=tiles, w=w):
            todo = []
            for u in range(DIL_UNROLL):
                c = cb * DIL_UNROLL + u
                r = c // tiles
                i = c % tiles
                q0 = pl.multiple_of(i * tq, tq)
                ks = pl.multiple_of(jnp.clip(q0 - tq // 2, 0, n - w), tq // 2)
                case = jnp.where(i == 0, 0, jnp.where(i == tiles - 1, 2, 1)) if tiles > 1 else 0
                sc = _band_scores(q_ref[0, 0, r, pl.ds(q0, tq), :], k_ref[0, 0, r, pl.ds(ks, w), :])
                todo.append((sc, r, q0, ks, case))
            for sc, r, q0, ks, case in todo:
                m_new, l_new, acc_new = _band_softmax_pv(sc, v_ref[0, 0, r, pl.ds(ks, w), :],
                                                         bias_ref[case, :, 0:w])
                if g == 0:
                    m_ref[pl.ds(q0, tq), :] = m_new
                    l_ref[pl.ds(q0, tq), :] = l_new
                    acc_ref[pl.ds(q0, tq), :] = acc_new
                else:
                    merge(pl.ds(q0 * d + r, tq, stride=d), m_new, l_new, acc_new)
            return carry

        assert (d * tiles) % DIL_UNROLL == 0
        lax.fori_loop(0, d * tiles // DIL_UNROLL, batch_step, 0)

    def out_step(i, carry):
        r0 = pl.multiple_of(i * 256, 256)
        o_ref[0, 0, pl.ds(r0, 256), :] = (acc_ref[pl.ds(r0, 256), :] / l_ref[pl.ds(r0, 256), :]).astype(BF16)
        return carry

    lax.fori_loop(0, s // 256, out_step, 0)


def _dilated_attention(qkv_groups):
    b, pairs, d0, s, _ = qkv_groups[0][0].shape
    assert d0 == 1
    halves = tuple(w // (2 * d) for (w, d) in DIL_GROUPS)
    dils = tuple(d for (_, d) in DIL_GROUPS)
    in_specs, args = [], []
    for q, k, v in qkv_groups:
        d, n = q.shape[2], q.shape[3]
        for arr in (q, k, v):
            in_specs.append(pl.BlockSpec((1, 1, d, n, LANES), lambda bi, p: (bi, p, 0, 0, 0)))
            args.append(arr)
    return pl.pallas_call(
        functools.partial(_dilated_attn_kernel, halves=halves, dils=dils),
        grid=(b, pairs),
        in_specs=in_specs,
        out_specs=pl.BlockSpec((1, 1, s, LANES), lambda bi, p: (bi, p, 0, 0)),
        out_shape=jax.ShapeDtypeStruct((b, pairs, s, LANES), BF16),
        scratch_shapes=[pltpu.VMEM((s, LANES), F32), pltpu.VMEM((s, LANES), F32), pltpu.VMEM((s, LANES), F32),
                        pltpu.VMEM((3, DIL_TQ, 2 * DIL_TQ), F32)],
        compiler_params=_params("parallel", "parallel"),
        name="dilated_attention",
    )(*args)


FFN_HALO = 16
FFN_UP_ROWS = 1024
FFN_DOWN_ROWS = 512
FFN_CHUNK = 256


def _ffn_kernel(h_ref, g_ref, wg_ref, wua_ref, wub_ref, cw_ref, cb_ref, wd_ref, hr_ref, fg_ref, o_ref,
                xn_ref, hid_ref, *, final_norm, n_chunks):
    j = pl.program_id(1)
    s = h_ref.shape[1]
    n_pairs = n_chunks // 2
    n_up = -(-n_chunks // 2)

    @pl.when(j == 0)
    def _():
        zeros = jnp.zeros((FFN_HALO, xn_ref.shape[1]), BF16)
        xn_ref[0:FFN_HALO, :] = zeros
        xn_ref[FFN_HALO + s:FFN_HALO + s + FFN_HALO, :] = zeros

        def norm_step(i, carry):
            r0 = pl.multiple_of(i * FFN_DOWN_ROWS, FFN_DOWN_ROWS)
            x = h_ref[0, pl.ds(r0, FFN_DOWN_ROWS), :]
            xn_ref[pl.ds(FFN_HALO + r0, FFN_DOWN_ROWS), :] = _rms_rows(x, g_ref[...], NORM_EPS).astype(BF16)
            return carry

        lax.fori_loop(0, s // FFN_DOWN_ROWS, norm_step, 0)

    def up_chunk(c, half, wu_ref):
        cols = slice(half * FFN_CHUNK, (half + 1) * FFN_CHUNK)
        inner = slice(FFN_HALO, FFN_HALO + FFN_UP_ROWS)
        wgu = jnp.concatenate([wg_ref[:, cols], wu_ref[...]], axis=-1)
        for rc in range(s // FFN_UP_ROWS):
            xs = xn_ref[rc * FFN_UP_ROWS:rc * FFN_UP_ROWS + FFN_UP_ROWS + 2 * FFN_HALO, :]
            gu = jnp.dot(xs, wgu, preferred_element_type=F32)
            g = gu[:, :FFN_CHUNK]
            gc = cb_ref[:, cols]
            for t in range(FFN_CONV):
                gc = gc + cw_ref[t:t + 1, cols] * _shift_rows(g, t - FFN_CONV // 2)[inner]
            act = 0.5 * gc * (1.0 + lax.erf(gc * (1.0 / math.sqrt(2.0))))
            hid_ref[c, rc * FFN_UP_ROWS:(rc + 1) * FFN_UP_ROWS, :] = (act * gu[inner, FFN_CHUNK:]).astype(BF16)

    @pl.when(j < n_pairs)
    def _():
        up_chunk(2 * j, 0, wua_ref)
        up_chunk(2 * j + 1, 1, wub_ref)

    if n_chunks % 2:
        @pl.when(j == n_pairs)
        def _():
            up_chunk(n_chunks - 1, 0, wua_ref)

    @pl.when(j >= n_up)
    def _():
        r0 = pl.multiple_of((j - n_up) * FFN_DOWN_ROWS, FFN_DOWN_ROWS)
        hid = jnp.concatenate([hid_ref[c, pl.ds(r0, FFN_DOWN_ROWS), :] for c in range(n_chunks)], axis=-1)
        out = hr_ref[0] + jnp.dot(hid, wd_ref[...], preferred_element_type=F32)
        if final_norm:
            out = _rms_rows(out, fg_ref[...], NORM_EPS)
        o_ref[0] = out


def _conv_ffn(h, gain, w_up, conv_w, conv_b, w_down, final_gain):
    b, s, dm = h.shape
    dff = w_down.shape[0]
    n_chunks = dff // FFN_CHUNK
    n_up = -(-n_chunks // 2)
    n_down = s // FFN_DOWN_ROWS
    assert dff % FFN_CHUNK == 0 and s % FFN_UP_ROWS == 0 and s % FFN_DOWN_ROWS == 0
    final_norm = final_gain is not None
    fg = (final_gain if final_norm else gain).reshape(1, dm)
    pad = 2 * n_up * FFN_CHUNK - dff
    conv_w = jnp.pad(conv_w, ((0, 0), (0, pad)))
    conv_b = jnp.pad(conv_b.reshape(1, dff), ((0, 0), (0, pad)))
    up_idx = lambda j: jnp.minimum(j, n_up - 1)
    down_idx = lambda j: jnp.maximum(j - n_up, 0)
    return pl.pallas_call(
        functools.partial(_ffn_kernel, final_norm=final_norm, n_chunks=n_chunks),
        grid=(b, n_up + n_down),
        in_specs=[
            pl.BlockSpec((1, s, dm), lambda bi, j: (bi, 0, 0)),
            pl.BlockSpec((1, dm), lambda bi, j: (0, 0)),
            pl.BlockSpec((dm, 2 * FFN_CHUNK), lambda bi, j: (0, up_idx(j))),
            pl.BlockSpec((dm, FFN_CHUNK), lambda bi, j: (0, n_chunks + 2 * up_idx(j))),
            pl.BlockSpec((dm, FFN_CHUNK), lambda bi, j: (0, jnp.minimum(n_chunks + 2 * up_idx(j) + 1,
                                                                         2 * n_chunks - 1))),
            pl.BlockSpec((FFN_CONV, 2 * FFN_CHUNK), lambda bi, j: (0, up_idx(j))),
            pl.BlockSpec((1, 2 * FFN_CHUNK), lambda bi, j: (0, up_idx(j))),
            pl.BlockSpec((dff, dm), lambda bi, j: (0, 0), pipeline_mode=pl.Buffered(1)),
            pl.BlockSpec((1, FFN_DOWN_ROWS, dm), lambda bi, j: (bi, down_idx(j), 0)),
            pl.BlockSpec((1, dm), lambda bi, j: (0, 0)),
        ],
        out_specs=pl.BlockSpec((1, FFN_DOWN_ROWS, dm), lambda bi, j: (bi, down_idx(j), 0)),
        out_shape=jax.ShapeDtypeStruct(h.shape, F32),
        scratch_shapes=[pltpu.VMEM((s + 2 * FFN_HALO, dm), BF16), pltpu.VMEM((n_chunks, s, FFN_CHUNK), BF16)],
        compiler_params=_params("parallel", "arbitrary"),
        name="conv_ffn",
    )(h, gain.reshape(1, dm), w_up, w_up, w_up, conv_w, conv_b, w_down, h, fg)


def kernel(x, positions, mix_norm, pool_w, pool_scale, diff_w_qkv, diff_lam_q1, diff_lam_k1, diff_lam_q2,
           diff_lam_k2, diff_subln, diff_w_o, lru_w_in, lru_conv_w, lru_conv_b, lru_w_a, lru_b_a, lru_w_x,
           lru_b_x, lru_lambda, lru_w_out, dil_w_qkv, dil_w_o, ffn_norm, ffn_w_up, ffn_conv_w, ffn_conv_b,
           ffn_w_down, final_norm):
    b, s, dm = x.shape
    depth = mix_norm.shape[0]
    cos, sin = _rope_tables(positions)
    h = x
    for i in range(depth):
        m, j = i % N_MIXERS, i // N_MIXERS
        if m == 0:
            h = _pool_mixer(h, mix_norm[i], pool_w[j], pool_scale[j])
        elif m == 1:
            q, k, v = _qkv_proj(h, mix_norm[i], diff_w_qkv[j].astype(BF16), 0, cos, sin, 1)
            lam_params = jnp.zeros((SUBLANES, LANES), F32)
            lam_params = lam_params.at[0:4, 0:HEAD_DIM].set(
                jnp.stack([diff_lam_q1[j], diff_lam_k1[j], diff_lam_q2[j], diff_lam_k2[j]]).astype(F32))
            lam_init = 0.8 - 0.6 * math.exp(-0.3 * i)
            o = _diff_attention(q, k, v, lam_params, diff_subln[j], lam_init)
            h = _out_proj_hm(o, diff_w_o[j].astype(BF16), h)
        elif m == 2:
            y = _lru_core(h, mix_norm[i], lru_w_in[j].astype(BF16), lru_conv_w[j], lru_conv_b[j], lru_w_a[j],
                          lru_b_a[j], lru_w_x[j], lru_b_x[j], lru_lambda[j])
            h = _out_proj(y, lru_w_out[j].astype(BF16), h)
        else:
            w = dil_w_qkv[j].astype(BF16)
            groups = [_qkv_proj(h, mix_norm[i], w, g * 3 * dm, cos, sin, d)
                      for g, (_, d) in enumerate(DIL_GROUPS)]
            o = _dilated_attention(groups)
            h = _out_proj_hm(o, dil_w_o[j].astype(BF16), h)
        h = _conv_ffn(h, ffn_norm[i], ffn_w_up[i].astype(BF16), ffn_conv_w[i], ffn_conv_b[i],
                      ffn_w_down[i].astype(BF16), final_norm if i == depth - 1 else None)
    return h
```

```python
import functools
import math

import jax
import jax.numpy as jnp
from jax import lax
from jax.experimental import pallas as pl
from jax.experimental.pallas import tpu as pltpu

F32 = jnp.float32
BF16 = jnp.bfloat16

HEAD_DIM = 64
NORM_EPS = 1e-6
ROPE_THETA = 10000.0
NEG_INF = -1e30
POOL_WINDOWS = (2, 4, 8, 16)
DIFF_SUBLN_EPS = 1e-5
LRU_C = 8.0
LRU_CONV = 4
DIL_GROUPS = ((128, 1), (512, 4), (2048, 16))
FFN_CONV = 3
N_MIXERS = 4

LANES = 128
SUBLANES = 8
VMEM_LIMIT_BYTES = 56 * 1024 * 1024


def _params(*semantics):
    return pltpu.CompilerParams(dimension_semantics=semantics, vmem_limit_bytes=VMEM_LIMIT_BYTES)


def _rms_rows(x, gain, eps):
    ms = jnp.mean(x * x, axis=-1, keepdims=True)
    return x * lax.rsqrt(ms + eps) * gain


def _halo_window(ref, r0, rows, cols=slice(None)):
    return ref[pl.ds(r0 - SUBLANES, rows + 2 * SUBLANES), cols]


def _shift_rows(win, off):
    if off == 0:
        return win
    return pltpu.roll(win, (-off) % win.shape[0], 0)


def _rope_table_kernel(pos_ref, inv_ref, cos_ref, sin_ref):
    ang = pos_ref[...] * inv_ref[...]
    lane = lax.broadcasted_iota(jnp.int32, ang.shape, 1)
    first_half = (lane % HEAD_DIM) < (HEAD_DIM // 2)
    cos_ref[...] = jnp.cos(ang)
    sin_ref[...] = jnp.where(first_half, -jnp.sin(ang), jnp.sin(ang))


def _rope_tables(positions):
    s = positions.shape[0]
    inv = ROPE_THETA ** (-jnp.arange(0, HEAD_DIM, 2, dtype=F32) / HEAD_DIM)
    inv = jnp.tile(inv, LANES // (HEAD_DIM // 2)).reshape(1, LANES)
    pos = positions.astype(F32).reshape(s, 1)
    return pl.pallas_call(
        _rope_table_kernel,
        out_shape=(jax.ShapeDtypeStruct((s, LANES), F32), jax.ShapeDtypeStruct((s, LANES), F32)),
        name="rope_tables",
    )(pos, inv)


Q_SCALE = HEAD_DIM ** -0.5 * math.log2(math.e)
QKV_TM = 1024
QKV_TN = 512


def _qkv_proj_kernel(x_ref, g_ref, wq_ref, wk_ref, wv_ref, cos_ref, sin_ref, q_ref, k_ref, v_ref, xn_ref,
                     *scratch, d):
    j = pl.program_id(2)
    tm, dm = x_ref.shape[1], x_ref.shape[2]
    nt = tm // d

    def chain_rows(ref, r):
        return ref[pl.ds(r, nt, stride=d), :]

    @pl.when(j == 0)
    def _():
        xnorm = _rms_rows(x_ref[0], g_ref[...], NORM_EPS)
        if d == 1:
            xn_ref[...] = xnorm.astype(BF16)
        else:
            xs_ref = scratch[0]
            for c in range(dm // LANES):
                xs_ref[c] = xnorm[:, c * LANES:(c + 1) * LANES]
            for r in range(d):
                for c in range(dm // LANES):
                    xn_ref[r * nt:(r + 1) * nt, c * LANES:(c + 1) * LANES] = chain_rows(xs_ref.at[c], r).astype(BF16)

    if d == 1:
        ct, st = cos_ref[...], sin_ref[...]
    else:
        ct = jnp.concatenate([chain_rows(cos_ref, r) for r in range(d)], axis=0)
        st = jnp.concatenate([chain_rows(sin_ref, r) for r in range(d)], axis=0)
    lane = lax.broadcasted_iota(jnp.int32, (1, LANES), 1)
    first_half = (lane % HEAD_DIM) < (HEAD_DIM // 2)
    xn = xn_ref[...]
    parts = ((wq_ref, q_ref), (wk_ref, k_ref), (wv_ref, v_ref))
    matmul = lambda part: jnp.dot(xn, parts[part][0][...], preferred_element_type=F32)
    pending = matmul(0)
    for part, (w_ref, o_ref) in enumerate(parts):
        y = pending
        pending = matmul(part + 1) if part + 1 < len(parts) else None
        for c in range(w_ref.shape[1] // LANES):
            yc = y[:, c * LANES:(c + 1) * LANES]
            if part < 2:
                partner = jnp.where(first_half, pltpu.roll(yc, LANES - HEAD_DIM // 2, 1),
                                    pltpu.roll(yc, HEAD_DIM // 2, 1))
                yc = yc * ct + partner * st
            if part == 0:
                yc = yc * Q_SCALE
            for r in range(d):
                o_ref[0, c, r] = yc[r * nt:(r + 1) * nt, :].astype(BF16)


def _qkv_proj(h, gain, w, col0, cos, sin, dilation):
    b, s, dm = h.shape
    d = dilation
    n = s // d
    tm, tn = QKV_TM, QKV_TN
    nt = tm // d
    assert s % tm == 0 and dm % tn == 0 and col0 % tn == 0 and nt % 16 == 0
    c0 = col0 // tn
    per = dm // tn
    out = jax.ShapeDtypeStruct((b, dm // LANES, d, n, LANES), BF16)
    out_spec = pl.BlockSpec((1, tn // LANES, d, nt, LANES), lambda bi, i, j: (bi, j, 0, i, 0))
    scratch = [pltpu.VMEM((tm, dm), BF16)]
    if d > 1:
        scratch.append(pltpu.VMEM((dm // LANES, tm, LANES), F32))
    return pl.pallas_call(
        functools.partial(_qkv_proj_kernel, d=d),
        grid=(b, s // tm, per),
        in_specs=[
            pl.BlockSpec((1, tm, dm), lambda bi, i, j: (bi, i, 0)),
            pl.BlockSpec((1, dm), lambda bi, i, j: (0, 0)),
            pl.BlockSpec((dm, tn), lambda bi, i, j: (0, c0 + j)),
            pl.BlockSpec((dm, tn), lambda bi, i, j: (0, c0 + per + j)),
            pl.BlockSpec((dm, tn), lambda bi, i, j: (0, c0 + 2 * per + j)),
            pl.BlockSpec((tm, LANES), lambda bi, i, j: (i, 0)),
            pl.BlockSpec((tm, LANES), lambda bi, i, j: (i, 0)),
        ],
        out_specs=(out_spec, out_spec, out_spec),
        out_shape=(out, out, out),
        scratch_shapes=scratch,
        compiler_params=_params("parallel", "parallel", "arbitrary"),
        name=f"qkv_proj_d{d}",
    )(h, gain.reshape(1, dm), w, w, w, cos, sin)


def _out_proj_hm_kernel(a_ref, w_ref, h_ref, o_ref):
    a = jnp.concatenate([a_ref[0, p] for p in range(a_ref.shape[1])], axis=-1)
    o_ref[0] = h_ref[0] + jnp.dot(a, w_ref[...], preferred_element_type=F32)


def _out_proj_hm(a_hm, w, h):
    b, s, dm = h.shape
    kb = a_hm.shape[1]
    tm = 1024
    return pl.pallas_call(
        _out_proj_hm_kernel,
        grid=(b, s // tm),
        in_specs=[
            pl.BlockSpec((1, kb, tm, LANES), lambda bi, i: (bi, 0, i, 0)),
            pl.BlockSpec(w.shape, lambda bi, i: (0, 0)),
            pl.BlockSpec((1, tm, dm), lambda bi, i: (bi, i, 0)),
        ],
        out_specs=pl.BlockSpec((1, tm, dm), lambda bi, i: (bi, i, 0)),
        out_shape=jax.ShapeDtypeStruct(h.shape, F32),
        compiler_params=_params("parallel", "parallel"),
        name="out_proj_hm",
    )(a_hm, w, h)


def _out_proj_kernel(a_ref, w_ref, h_ref, o_ref):
    o_ref[0] = h_ref[0] + jnp.dot(a_ref[0], w_ref[...], preferred_element_type=F32)


def _out_proj(a, w, h):
    b, s, dm = h.shape
    k = a.shape[-1]
    tm = 1024
    return pl.pallas_call(
        _out_proj_kernel,
        grid=(b, s // tm),
        in_specs=[
            pl.BlockSpec((1, tm, k), lambda bi, i: (bi, i, 0)),
            pl.BlockSpec(w.shape, lambda bi, i: (0, 0)),
            pl.BlockSpec((1, tm, dm), lambda bi, i: (bi, i, 0)),
        ],
        out_specs=pl.BlockSpec((1, tm, dm), lambda bi, i: (bi, i, 0)),
        out_shape=jax.ShapeDtypeStruct(h.shape, F32),
        compiler_params=_params("parallel", "parallel"),
        name="out_proj",
    )(a, w, h)


POOL_PAD = 8
POOL_ROWS = 256


def _pool_kernel(h_ref, g_ref, w_ref, sc_ref, o_ref, hn_ref):
    s, dm = h_ref.shape[1], h_ref.shape[2]
    group = dm // len(POOL_WINDOWS)
    zeros = jnp.zeros((POOL_PAD, dm), F32)
    hn_ref[0:POOL_PAD, :] = zeros
    hn_ref[POOL_PAD + s:POOL_PAD + s + POOL_PAD, :] = zeros

    def norm_step(i, carry):
        r0 = pl.multiple_of(i * POOL_ROWS, POOL_ROWS)
        hn_ref[pl.ds(POOL_PAD + r0, POOL_ROWS), :] = _rms_rows(h_ref[0, pl.ds(r0, POOL_ROWS), :], g_ref[...], NORM_EPS)
        return carry

    lax.fori_loop(0, s // POOL_ROWS, norm_step, 0)

    def mix_step(i, carry):
        r0 = pl.multiple_of(i * POOL_ROWS, POOL_ROWS)
        t = r0 + lax.broadcasted_iota(jnp.int32, (POOL_ROWS, 1), 0)
        inner = slice(SUBLANES, SUBLANES + POOL_ROWS)
        for g, win in enumerate(POOL_WINDOWS):
            half = win // 2
            sl = slice(g * group, (g + 1) * group)
            xw = _halo_window(hn_ref, POOL_PAD + r0, POOL_ROWS, sl)
            tot = xw + _shift_rows(xw, -1)
            step = 1
            while 2 * step < win:
                tot = _shift_rows(tot, -step) + _shift_rows(tot, step)
                step *= 2
            cnt = (jnp.minimum(t + half, s) - jnp.maximum(t - half, 0)).astype(F32)
            pooled = (tot[inner] / cnt - xw[inner]).astype(BF16)
            y = jnp.dot(pooled, w_ref[g], preferred_element_type=F32) * sc_ref[:, sl]
            o_ref[0, pl.ds(r0, POOL_ROWS), sl] = h_ref[0, pl.ds(r0, POOL_ROWS), sl] + y
        return carry

    lax.fori_loop(0, s // POOL_ROWS, mix_step, 0)


def _pool_mixer(h, gain, w, scale):
    b, s, dm = h.shape
    return pl.pallas_call(
        _pool_kernel,
        grid=(b,),
        in_specs=[
            pl.BlockSpec((1, s, dm), lambda bi: (bi, 0, 0)),
            pl.BlockSpec((1, dm), lambda bi: (0, 0)),
            pl.BlockSpec(w.shape, lambda bi: (0, 0, 0)),
            pl.BlockSpec((1, dm), lambda bi: (0, 0)),
        ],
        out_specs=pl.BlockSpec((1, s, dm), lambda bi: (bi, 0, 0)),
        out_shape=jax.ShapeDtypeStruct(h.shape, F32),
        scratch_shapes=[pltpu.VMEM((s + 2 * POOL_PAD, dm), F32)],
        compiler_params=_params("parallel"),
        name="pool_mixer",
    )(h, gain.reshape(1, dm), w.astype(BF16), scale.reshape(1, dm))


DIFF_TQ = 256
DIFF_CHAINS = 4
DIFF_AHEAD = 2


def _diff_attn_kernel(q_ref, k_ref, v_ref, lam_ref, sub_ref, o_ref, *, lam_init):
    k = k_ref[0, 0, 0]
    v = v_ref[0, 0, 0]
    lp = lam_ref[...]
    lam = (jnp.exp(jnp.sum(lp[0:1] * lp[1:2], axis=-1, keepdims=True))
           - jnp.exp(jnp.sum(lp[2:3] * lp[3:4], axis=-1, keepdims=True)) + lam_init)
    lane = lax.broadcasted_iota(jnp.int32, (1, LANES), 1)
    first = lane < HEAD_DIM

    def scores(c):
        q = q_ref[0, 0, 0, c * DIFF_TQ:(c + 1) * DIFF_TQ, :]
        zero = jnp.zeros_like(q)
        dn = (((1,), (1,)), ((), ()))
        return (lax.dot_general(jnp.where(first, q, zero), k, dn, preferred_element_type=F32),
                lax.dot_general(jnp.where(first, zero, q), k, dn, preferred_element_type=F32))

    def softmax_parts(sc):
        p = jnp.exp2(sc - jnp.max(sc, axis=-1, keepdims=True))
        return p.astype(BF16), jnp.sum(p, axis=-1, keepdims=True)

    pending = [scores(c) for c in range(min(DIFF_AHEAD, DIFF_CHAINS))]
    for c in range(DIFF_CHAINS):
        if c + DIFF_AHEAD < DIFF_CHAINS:
            pending.append(scores(c + DIFF_AHEAD))
        sc = pending.pop(0)
        p0, l0 = softmax_parts(sc[0])
        p1, l1 = softmax_parts(sc[1])
        a = p0 * (1.0 / l0).astype(BF16) - p1 * (lam / l1).astype(BF16)
        o = jnp.dot(a, v, preferred_element_type=F32)
        o = _rms_rows(o, sub_ref[...], DIFF_SUBLN_EPS) * (1.0 - lam_init)
        o_ref[0, 0, c * DIFF_TQ:(c + 1) * DIFF_TQ, :] = o.astype(BF16)


def _diff_attention(q, k, v, lam_params, subln, lam_init):
    b, heads, _, s, _ = q.shape
    tq = DIFF_TQ * DIFF_CHAINS
    return pl.pallas_call(
        functools.partial(_diff_attn_kernel, lam_init=lam_init),
        grid=(b, heads, s // tq),
        in_specs=[
            pl.BlockSpec((1, 1, 1, tq, LANES), lambda bi, h, i: (bi, h, 0, i, 0)),
            pl.BlockSpec((1, 1, 1, s, LANES), lambda bi, h, i: (bi, h, 0, 0, 0)),
            pl.BlockSpec((1, 1, 1, s, LANES), lambda bi, h, i: (bi, h, 0, 0, 0)),
            pl.BlockSpec(lam_params.shape, lambda bi, h, i: (0, 0)),
            pl.BlockSpec((1, LANES), lambda bi, h, i: (0, 0)),
        ],
        out_specs=pl.BlockSpec((1, 1, tq, LANES), lambda bi, h, i: (bi, h, i, 0)),
        out_shape=jax.ShapeDtypeStruct((b, heads, s, LANES), BF16),
        compiler_params=_params("parallel", "parallel", "parallel"),
        name="diff_attention",
    )(q, k, v, lam_params, subln.reshape(1, LANES))


LRU_PAD = 8
LRU_ROWS = 256
LRU_PROJ_ROWS = 1024
LRU_CW = 256
LRU_SCAN_TILES = 8


def _log_sigmoid(x):
    return jnp.minimum(x, 0.0) - jnp.log1p(jnp.exp(-jnp.abs(x)))


def _sigmoid(x):
    return 0.5 * jnp.tanh(0.5 * x) + 0.5


def _gelu_tanh(x):
    return 0.5 * x * (1.0 + jnp.tanh(math.sqrt(2.0 / math.pi) * (x + 0.044715 * (x * x * x))))


def _lru_kernel(h_ref, g_ref, wg_ref, wu_ref, cw_ref, cb_ref, wbd_ref, bias_ref, lam_ref, o_ref,
                xn_ref, gate_ref, upad_ref, a_ref, b_ref, c_ref):
    s = h_ref.shape[1]
    cw = wg_ref.shape[1]
    n_slab = cw // LANES
    assert s == SUBLANES * LRU_ROWS

    @pl.when(pl.program_id(1) == 0)
    def _():
        def norm_step(i, carry):
            r0 = pl.multiple_of(i * LRU_ROWS, LRU_ROWS)
            xn_ref[pl.ds(r0, LRU_ROWS), :] = _rms_rows(h_ref[0, pl.ds(r0, LRU_ROWS), :], g_ref[...],
                                                       NORM_EPS).astype(BF16)
            return carry

        lax.fori_loop(0, s // LRU_ROWS, norm_step, 0)

    zeros = jnp.zeros((LRU_PAD, cw), F32)
    upad_ref[0:LRU_PAD, :] = zeros
    upad_ref[LRU_PAD + s:LRU_PAD + s + LRU_PAD, :] = zeros
    wgu = jnp.concatenate([wg_ref[...], wu_ref[...]], axis=-1)
    for blk in range(s // LRU_PROJ_ROWS):
        rows = slice(blk * LRU_PROJ_ROWS, (blk + 1) * LRU_PROJ_ROWS)
        gu = jnp.dot(xn_ref[rows, :], wgu, preferred_element_type=F32)
        gate_ref[rows, :] = gu[:, :cw]
        upad_ref[LRU_PAD + blk * LRU_PROJ_ROWS:LRU_PAD + (blk + 1) * LRU_PROJ_ROWS, :] = gu[:, cw:]

    row8 = lax.broadcasted_iota(jnp.int32, (SUBLANES, 1), 0)
    for dr in range(2):
        log_a_unit = LRU_C * _log_sigmoid(lam_ref[dr:dr + 1, :])

        def gate_step(i, carry, dr=dr, log_a_unit=log_a_unit):
            r0 = pl.multiple_of(i * LRU_ROWS, LRU_ROWS)
            uw = _halo_window(upad_ref, LRU_PAD + r0, LRU_ROWS)
            xc = cb_ref[dr:dr + 1, :]
            for j in range(LRU_CONV):
                off = (j - (LRU_CONV - 1)) if dr == 0 else ((LRU_CONV - 1) - j)
                xc = xc + cw_ref[dr, j:j + 1, :] * _shift_rows(uw, off)[SUBLANES:SUBLANES + LRU_ROWS]
            pre = jnp.dot(xc.astype(BF16), wbd_ref[dr, 0], preferred_element_type=F32) + bias_ref[0, dr:dr + 1, :]
            r = _sigmoid(pre[:, :cw])
            ig = _sigmoid(pre[:, cw:])
            log_a = r * log_a_unit
            a = jnp.exp(log_a)
            th = jnp.tanh(log_a)
            bv = jnp.sqrt(-2.0 * th / (1.0 - th)) * (ig * xc)
            for sl in range(n_slab):
                lanes = slice(sl * LANES, (sl + 1) * LANES)
                a_ref[dr, sl, pl.ds(i, LRU_ROWS, stride=SUBLANES), :] = a[:, lanes]
                b_ref[dr, sl, pl.ds(i, LRU_ROWS, stride=SUBLANES), :] = bv[:, lanes]
            return carry

        lax.fori_loop(0, SUBLANES, gate_step, 0)

        n_blocks = LRU_ROWS // LRU_SCAN_TILES

        def scan_block(tb, carry, dr=dr):
            blk = tb if dr == 0 else n_blocks - 1 - tb
            base = pl.multiple_of(blk * (LRU_SCAN_TILES * SUBLANES), LRU_SCAN_TILES * SUBLANES)
            hs, ps = list(carry[0]), list(carry[1])
            for u in range(LRU_SCAN_TILES):
                tile = u if dr == 0 else LRU_SCAN_TILES - 1 - u
                rows = pl.ds(base + tile * SUBLANES, SUBLANES)
                for sl in range(n_slab):
                    a_t = a_ref[dr, sl, rows, :]
                    hs[sl] = a_t * hs[sl] + b_ref[dr, sl, rows, :]
                    ps[sl] = a_t * ps[sl]
                    b_ref[dr, sl, rows, :] = hs[sl]
                    a_ref[dr, sl, rows, :] = ps[sl]
            return tuple(hs), tuple(ps)

        init = (tuple(jnp.zeros((SUBLANES, LANES), F32) for _ in range(n_slab)),
                tuple(jnp.ones((SUBLANES, LANES), F32) for _ in range(n_slab)))
        h_end, p_end = lax.fori_loop(0, n_blocks, scan_block, init)

        for sl in range(n_slab):
            c = jnp.zeros((SUBLANES, LANES), F32)
            for _ in range(SUBLANES - 1):
                nxt = h_end[sl] + p_end[sl] * c
                if dr == 0:
                    c = jnp.where(row8 == 0, 0.0, _shift_rows(nxt, -1))
                else:
                    c = jnp.where(row8 == SUBLANES - 1, 0.0, _shift_rows(nxt, 1))
            c_ref[dr, sl] = c

    def fix_step(t, carry):
        r0 = pl.multiple_of(t * LRU_ROWS, LRU_ROWS)
        for sl in range(n_slab):
            tot = None
            for dr in range(2):
                local = b_ref[dr, sl, pl.ds(r0, LRU_ROWS), :].reshape(-1, SUBLANES, LANES)
                decay = a_ref[dr, sl, pl.ds(r0, LRU_ROWS), :].reshape(-1, SUBLANES, LANES)
                fixed = local + decay * c_ref[dr, sl]
                tot = fixed if tot is None else tot + fixed
            b_ref[0, sl, pl.ds(r0, LRU_ROWS), :] = tot.reshape(LRU_ROWS, LANES)
        return carry

    lax.fori_loop(0, SUBLANES, fix_step, 0)

    for i in range(SUBLANES):
        rows = slice(i * LRU_ROWS, (i + 1) * LRU_ROWS)
        hsum = jnp.concatenate([b_ref[0, sl, pl.ds(i, LRU_ROWS, stride=SUBLANES), :] for sl in range(n_slab)],
                               axis=-1)
        o_ref[0, rows, :] = (_gelu_tanh(gate_ref[rows, :]) * hsum).astype(BF16)


def _lru_core(h, gain, w_in, conv_w, conv_b, w_a, b_a, w_x, b_x, lam):
    b, s, dm = h.shape
    c = w_in.shape[1] // 2
    cw = LRU_CW
    ncb = c // cw
    per = cw // HEAD_DIM
    eye = jnp.eye(per, dtype=F32)

    def block_diag(w):
        w5 = w.reshape(2, ncb, per, HEAD_DIM, HEAD_DIM)
        return jnp.einsum('dcipq,ik->dcipkq', w5, eye).reshape(2, ncb, cw, cw)

    wbd = jnp.concatenate([block_diag(w_a), block_diag(w_x)], axis=-1).astype(BF16)
    bias = jnp.concatenate([b_a.reshape(2, ncb, cw), b_x.reshape(2, ncb, cw)], axis=-1)
    bias = bias.transpose(1, 0, 2)
    n_slab = cw // LANES
    return pl.pallas_call(
        _lru_kernel,
        grid=(b, ncb),
        in_specs=[
            pl.BlockSpec((1, s, dm), lambda bi, cb: (bi, 0, 0), pipeline_mode=pl.Buffered(1)),
            pl.BlockSpec((1, dm), lambda bi, cb: (0, 0)),
            pl.BlockSpec((dm, cw), lambda bi, cb: (0, cb)),
            pl.BlockSpec((dm, cw), lambda bi, cb: (0, ncb + cb)),
            pl.BlockSpec((2, LRU_CONV, cw), lambda bi, cb: (0, 0, cb)),
            pl.BlockSpec((2, cw), lambda bi, cb: (0, cb)),
            pl.BlockSpec((2, 1, cw, 2 * cw), lambda bi, cb: (0, cb, 0, 0)),
            pl.BlockSpec((1, 2, 2 * cw), lambda bi, cb: (cb, 0, 0)),
            pl.BlockSpec((2, cw), lambda bi, cb: (0, cb)),
        ],
        out_specs=pl.BlockSpec((1, s, cw), lambda bi, cb: (bi, 0, cb)),
        out_shape=jax.ShapeDtypeStruct((b, s, c), BF16),
        scratch_shapes=[
            pltpu.VMEM((s, dm), BF16),
            pltpu.VMEM((s, cw), F32),
            pltpu.VMEM((s + 2 * LRU_PAD, cw), F32),
            pltpu.VMEM((2, n_slab, s, LANES), F32),
            pltpu.VMEM((2, n_slab, s, LANES), F32),
            pltpu.VMEM((2, n_slab, SUBLANES, LANES), F32),
        ],
        compiler_params=_params("parallel", "arbitrary"),
        name="lru_core",
    )(h, gain.reshape(1, dm), w_in, w_in, conv_w, conv_b, wbd, bias, lam)


DIL_TQ = 128
DIL_UNROLL = 4


def _band_bias(tq, w, rel0, half):
    qi = lax.broadcasted_iota(jnp.int32, (tq, 1), 0)
    ki = lax.broadcasted_iota(jnp.int32, (1, w), 1)
    return jnp.where(jnp.abs(ki + rel0 - qi) <= half, 0.0, NEG_INF).astype(F32)


def _band_scores(q, k):
    lane = lax.broadcasted_iota(jnp.int32, (1, LANES), 1)
    first = lane < HEAD_DIM
    zero = jnp.zeros_like(q)
    qq = jnp.concatenate([jnp.where(first, q, zero), jnp.where(first, zero, q)], axis=0)
    return lax.dot_general(qq, k, (((1,), (1,)), ((), ())), preferred_element_type=F32)


def _band_softmax_pv(sc, v, bias):
    tq = sc.shape[0] // 2
    lane = lax.broadcasted_iota(jnp.int32, (1, LANES), 1)
    first = lane < HEAD_DIM
    sc = sc + jnp.concatenate([bias, bias], axis=0)
    m = jnp.max(sc, axis=-1, keepdims=True)
    p = jnp.exp2(sc - m).astype(BF16)
    v1 = jnp.concatenate([v, jnp.ones_like(v)], axis=-1)
    pv = jnp.dot(p, v1, preferred_element_type=F32)
    m_b = jnp.broadcast_to(m, (2 * tq, LANES))
    pick = lambda t: jnp.where(first, t[:tq], t[tq:])
    return pick(m_b), pick(pv[:, LANES:]), pick(pv[:, :LANES])


def _dilated_attn_kernel(q0_ref, k0_ref, v0_ref, q1_ref, k1_ref, v1_ref, q2_ref, k2_ref, v2_ref,
                         o_ref, m_ref, l_ref, acc_ref, bias_ref, *, halves, dils):
    s = o_ref.shape[2]
    tq = DIL_TQ
    win = 2 * tq

    def merge(rows, m_new, l_new, acc_new):
        m_old, l_old, acc_old = m_ref[rows, :], l_ref[rows, :], acc_ref[rows, :]
        m_tot = jnp.maximum(m_old, m_new)
        alpha = jnp.exp2(m_old - m_tot)
        beta = jnp.exp2(m_new - m_tot)
        m_ref[rows, :] = m_tot
        l_ref[rows, :] = alpha * l_old + beta * l_new
        acc_ref[rows, :] = alpha * acc_old + beta * acc_new

    groups = ((q0_ref, k0_ref, v0_ref), (q1_ref, k1_ref, v1_ref), (q2_ref, k2_ref, v2_ref))
    order = sorted(range(len(groups)), key=lambda gi: -dils[gi])
    for g in order:
        q_ref, k_ref, v_ref = groups[g]
        half, d = halves[g], dils[g]
        n = s // d
        tiles = n // tq
        w = min(win, n)
        for case, rel0 in enumerate((0, -(tq // 2), tq - w) if tiles > 1 else (0,)):
            bias_ref[case, :, 0:w] = _band_bias(tq, w, rel0, half)

        def batch_step(cb, carry, g=g, q_ref=q_ref, k_ref=k_ref, v_ref=v_ref, d=d, n=n, tiles=tiles, w=w):
            todo = []
            for u in range(DIL_UNROLL):
                c = cb * DIL_UNROLL + u
                r = c // tiles
                i = c % tiles
                q0 = pl.multiple_of(i * tq, tq)
                ks = pl.multiple_of(jnp.clip(q0 - tq // 2, 0, n - w), tq // 2)
                case = jnp.where(i == 0, 0, jnp.where(i == tiles - 1, 2, 1)) if tiles > 1 else 0
                sc = _band_scores(q_ref[0, 0, r, pl.ds(q0, tq), :], k_ref[0, 0, r, pl.ds(ks, w), :])
                todo.append((sc, r, q0, ks, case))
            for sc, r, q0, ks, case in todo:
                m_new, l_new, acc_new = _band_softmax_pv(sc, v_ref[0, 0, r, pl.ds(ks, w), :],
                                                         bias_ref[case, :, 0:w])
                rows = pl.ds(q0, tq) if d == 1 else pl.ds(q0 * d + r, tq, stride=d)
                if g == order[0]:
                    m_ref[rows, :] = m_new
                    l_ref[rows, :] = l_new
                    acc_ref[rows, :] = acc_new
                else:
                    merge(rows, m_new, l_new, acc_new)
            return carry

        assert (d * tiles) % DIL_UNROLL == 0
        lax.fori_loop(0, d * tiles // DIL_UNROLL, batch_step, 0)

    def out_step(i, carry):
        r0 = pl.multiple_of(i * 256, 256)
        o_ref[0, 0, pl.ds(r0, 256), :] = (acc_ref[pl.ds(r0, 256), :] / l_ref[pl.ds(r0, 256), :]).astype(BF16)
        return carry

    lax.fori_loop(0, s // 256, out_step, 0)


def _dilated_attention(qkv_groups):
    b, pairs, d0, s, _ = qkv_groups[0][0].shape
    assert d0 == 1
    halves = tuple(w // (2 * d) for (w, d) in DIL_GROUPS)
    dils = tuple(d for (_, d) in DIL_GROUPS)
    in_specs, args = [], []
    for q, k, v in qkv_groups:
        d, n = q.shape[2], q.shape[3]
        for arr in (q, k, v):
            in_specs.append(pl.BlockSpec((1, 1, d, n, LANES), lambda bi, p: (bi, p, 0, 0, 0)))
            args.append(arr)
    return pl.pallas_call(
        functools.partial(_dilated_attn_kernel, halves=halves, dils=dils),
        grid=(b, pairs),
        in_specs=in_specs,
        out_specs=pl.BlockSpec((1, 1, s, LANES), lambda bi, p: (bi, p, 0, 0)),
        out_shape=jax.ShapeDtypeStruct((b, pairs, s, LANES), BF16),
        scratch_shapes=[pltpu.VMEM((s, LANES), F32), pltpu.VMEM((s, LANES), F32), pltpu.VMEM((s, LANES), F32),
                        pltpu.VMEM((3, DIL_TQ, 2 * DIL_TQ), F32)],
        compiler_params=_params("parallel", "parallel"),
        name="dilated_attention",
    )(*args)


FFN_HALO = 16
FFN_UP_ROWS = 1024
FFN_DOWN_ROWS = 512
FFN_CHUNK = 256


def _ffn_kernel(h_ref, g_ref, wg_ref, wua_ref, wub_ref, cw_ref, cb_ref, wd_ref, hr_ref, fg_ref, o_ref,
                xn_ref, hid_ref, *, final_norm, n_chunks):
    j = pl.program_id(1)
    s = h_ref.shape[1]
    n_pairs = n_chunks // 2
    n_up = -(-n_chunks // 2)

    @pl.when(j == 0)
    def _():
        zeros = jnp.zeros((FFN_HALO, xn_ref.shape[1]), BF16)
        xn_ref[0:FFN_HALO, :] = zeros
        xn_ref[FFN_HALO + s:FFN_HALO + s + FFN_HALO, :] = zeros

        def norm_step(i, carry):
            r0 = pl.multiple_of(i * FFN_DOWN_ROWS, FFN_DOWN_ROWS)
            x = h_ref[0, pl.ds(r0, FFN_DOWN_ROWS), :]
            xn_ref[pl.ds(FFN_HALO + r0, FFN_DOWN_ROWS), :] = _rms_rows(x, g_ref[...], NORM_EPS).astype(BF16)
            return carry

        lax.fori_loop(0, s // FFN_DOWN_ROWS, norm_step, 0)

    def up_chunks(chunks):
        inner = slice(FFN_HALO, FFN_HALO + FFN_UP_ROWS)
        jobs = []
        for c, half, wu_ref in chunks:
            cols = slice(half * FFN_CHUNK, (half + 1) * FFN_CHUNK)
            wgu = jnp.concatenate([wg_ref[:, cols], wu_ref[...]], axis=-1)
            jobs += [(c, cols, wgu, rc) for rc in range(s // FFN_UP_ROWS)]

        def matmul(job):
            _, _, wgu, rc = job
            xs = xn_ref[rc * FFN_UP_ROWS:rc * FFN_UP_ROWS + FFN_UP_ROWS + 2 * FFN_HALO, :]
            return jnp.dot(xs, wgu, preferred_element_type=F32)

        def epilogue(job, gu):
            c, cols, _, rc = job
            g = gu[:, :FFN_CHUNK]
            gc = cb_ref[:, cols]
            for t in range(FFN_CONV):
                gc = gc + cw_ref[t:t + 1, cols] * _shift_rows(g, t - FFN_CONV // 2)[inner]
            act = 0.5 * gc * (1.0 + lax.erf(gc * (1.0 / math.sqrt(2.0))))
            hid_ref[c, rc * FFN_UP_ROWS:(rc + 1) * FFN_UP_ROWS, :] = (act * gu[inner, FFN_CHUNK:]).astype(BF16)

        pending = matmul(jobs[0])
        for n, job in enumerate(jobs):
            nxt = matmul(jobs[n + 1]) if n + 1 < len(jobs) else None
            epilogue(job, pending)
            pending = nxt

    @pl.when(j < n_pairs)
    def _():
        up_chunks([(2 * j, 0, wua_ref), (2 * j + 1, 1, wub_ref)])

    if n_chunks % 2:
        @pl.when(j == n_pairs)
        def _():
            up_chunks([(n_chunks - 1, 0, wua_ref)])

    @pl.when(j >= n_up)
    def _():
        r0 = pl.multiple_of((j - n_up) * FFN_DOWN_ROWS, FFN_DOWN_ROWS)
        hid = jnp.concatenate([hid_ref[c, pl.ds(r0, FFN_DOWN_ROWS), :] for c in range(n_chunks)], axis=-1)
        out = hr_ref[0] + jnp.dot(hid, wd_ref[...], preferred_element_type=F32)
        if final_norm:
            out = _rms_rows(out, fg_ref[...], NORM_EPS)
        o_ref[0] = out


def _conv_ffn(h, gain, w_up, conv_w, conv_b, w_down, final_gain):
    b, s, dm = h.shape
    dff = w_down.shape[0]
    n_chunks = dff // FFN_CHUNK
    n_up = -(-n_chunks // 2)
    n_down = s // FFN_DOWN_ROWS
    assert dff % FFN_CHUNK == 0 and s % FFN_UP_ROWS == 0 and s % FFN_DOWN_ROWS == 0
    final_norm = final_gain is not None
    fg = (final_gain if final_norm else gain).reshape(1, dm)
    pad = 2 * n_up * FFN_CHUNK - dff
    conv_w = jnp.pad(conv_w, ((0, 0), (0, pad)))
    conv_b = jnp.pad(conv_b.reshape(1, dff), ((0, 0), (0, pad)))
    up_idx = lambda j: jnp.minimum(j, n_up - 1)
    down_idx = lambda j: jnp.maximum(j - n_up, 0)
    return pl.pallas_call(
        functools.partial(_ffn_kernel, final_norm=final_norm, n_chunks=n_chunks),
        grid=(b, n_up + n_down),
        in_specs=[
            pl.BlockSpec((1, s, dm), lambda bi, j: (bi, 0, 0)),
            pl.BlockSpec((1, dm), lambda bi, j: (0, 0)),
            pl.BlockSpec((dm, 2 * FFN_CHUNK), lambda bi, j: (0, up_idx(j))),
            pl.BlockSpec((dm, FFN_CHUNK), lambda bi, j: (0, n_chunks + 2 * up_idx(j))),
            pl.BlockSpec((dm, FFN_CHUNK), lambda bi, j: (0, jnp.minimum(n_chunks + 2 * up_idx(j) + 1,
                                                                         2 * n_chunks - 1))),
            pl.BlockSpec((FFN_CONV, 2 * FFN_CHUNK), lambda bi, j: (0, up_idx(j))),
            pl.BlockSpec((1, 2 * FFN_CHUNK), lambda bi, j: (0, up_idx(j))),
            pl.BlockSpec((dff, dm), lambda bi, j: (0, 0), pipeline_mode=pl.Buffered(1)),
            pl.BlockSpec((1, FFN_DOWN_ROWS, dm), lambda bi, j: (bi, down_idx(j), 0)),
            pl.BlockSpec((1, dm), lambda bi, j: (0, 0)),
        ],
        out_specs=pl.BlockSpec((1, FFN_DOWN_ROWS, dm), lambda bi, j: (bi, down_idx(j), 0)),
        out_shape=jax.ShapeDtypeStruct(h.shape, F32),
        scratch_shapes=[pltpu.VMEM((s + 2 * FFN_HALO, dm), BF16), pltpu.VMEM((n_chunks, s, FFN_CHUNK), BF16)],
        compiler_params=_params("parallel", "arbitrary"),
        name="conv_ffn",
    )(h, gain.reshape(1, dm), w_up, w_up, w_up, conv_w, conv_b, w_down, h, fg)


def kernel(x, positions, mix_norm, pool_w, pool_scale, diff_w_qkv, diff_lam_q1, diff_lam_k1, diff_lam_q2,
           diff_lam_k2, diff_subln, diff_w_o, lru_w_in, lru_conv_w, lru_conv_b, lru_w_a, lru_b_a, lru_w_x,
           lru_b_x, lru_lambda, lru_w_out, dil_w_qkv, dil_w_o, ffn_norm, ffn_w_up, ffn_conv_w, ffn_conv_b,
           ffn_w_down, final_norm):
    b, s, dm = x.shape
    depth = mix_norm.shape[0]
    cos, sin = _rope_tables(positions)
    h = x
    for i in range(depth):
        m, j = i % N_MIXERS, i // N_MIXERS
        if m == 0:
            h = _pool_mixer(h, mix_norm[i], pool_w[j], pool_scale[j])
        elif m == 1:
            q, k, v = _qkv_proj(h, mix_norm[i], diff_w_qkv[j].astype(BF16), 0, cos, sin, 1)
            lam_params = jnp.zeros((SUBLANES, LANES), F32)
            lam_params = lam_params.at[0:4, 0:HEAD_DIM].set(
                jnp.stack([diff_lam_q1[j], diff_lam_k1[j], diff_lam_q2[j], diff_lam_k2[j]]).astype(F32))
            lam_init = 0.8 - 0.6 * math.exp(-0.3 * i)
            o = _diff_attention(q, k, v, lam_params, diff_subln[j], lam_init)
            h = _out_proj_hm(o, diff_w_o[j].astype(BF16), h)
        elif m == 2:
            y = _lru_core(h, mix_norm[i], lru_w_in[j].astype(BF16), lru_conv_w[j], lru_conv_b[j], lru_w_a[j],
                          lru_b_a[j], lru_w_x[j], lru_b_x[j], lru_lambda[j])
            h = _out_proj(y, lru_w_out[j].astype(BF16), h)
        else:
            w = dil_w_qkv[j].astype(BF16)
            groups = [_qkv_proj(h, mix_norm[i], w, g * 3 * dm, cos, sin, d)
                      for g, (_, d) in enumerate(DIL_GROUPS)]
            o = _dilated_attention(groups)
            h = _out_proj_hm(o, dil_w_o[j].astype(BF16), h)
        h = _conv_ffn(h, ffn_norm[i], ffn_w_up[i].astype(BF16), ffn_conv_w[i], ffn_conv_b[i],
                      ffn_w_down[i].astype(BF16), final_norm if i == depth - 1 else None)
    return h
```

```python
import functools
import math

import jax
import jax.numpy as jnp
from jax import lax
from jax.experimental import pallas as pl
from jax.experimental.pallas import tpu as pltpu

F32 = jnp.float32
BF16 = jnp.bfloat16

HEAD_DIM = 64
NORM_EPS = 1e-6
ROPE_THETA = 10000.0
NEG_INF = -1e30
POOL_WINDOWS = (2, 4, 8, 16)
DIFF_SUBLN_EPS = 1e-5
LRU_C = 8.0
LRU_CONV = 4
DIL_GROUPS = ((128, 1), (512, 4), (2048, 16))
FFN_CONV = 3
N_MIXERS = 4

LANES = 128
SUBLANES = 8
VMEM_LIMIT_BYTES = 56 * 1024 * 1024


def _params(*semantics):
    return pltpu.CompilerParams(dimension_semantics=semantics, vmem_limit_bytes=VMEM_LIMIT_BYTES)


def _rms_rows(x, gain, eps):
    ms = jnp.mean(x * x, axis=-1, keepdims=True)
    return x * lax.rsqrt(ms + eps) * gain


def _halo_window(ref, r0, rows, cols=slice(None)):
    return ref[pl.ds(r0 - SUBLANES, rows + 2 * SUBLANES), cols]


def _shift_rows(win, off):
    if off == 0:
        return win
    return pltpu.roll(win, (-off) % win.shape[0], 0)


def _rope_table_kernel(pos_ref, inv_ref, cos_ref, sin_ref):
    ang = pos_ref[...] * inv_ref[...]
    lane = lax.broadcasted_iota(jnp.int32, ang.shape, 1)
    first_half = (lane % HEAD_DIM) < (HEAD_DIM // 2)
    cos_ref[...] = jnp.cos(ang)
    sin_ref[...] = jnp.where(first_half, -jnp.sin(ang), jnp.sin(ang))


def _rope_tables(positions):
    s = positions.shape[0]
    inv = ROPE_THETA ** (-jnp.arange(0, HEAD_DIM, 2, dtype=F32) / HEAD_DIM)
    inv = jnp.tile(inv, LANES // (HEAD_DIM // 2)).reshape(1, LANES)
    pos = positions.astype(F32).reshape(s, 1)
    return pl.pallas_call(
        _rope_table_kernel,
        out_shape=(jax.ShapeDtypeStruct((s, LANES), F32), jax.ShapeDtypeStruct((s, LANES), F32)),
        name="rope_tables",
    )(pos, inv)


Q_SCALE = HEAD_DIM ** -0.5 * math.log2(math.e)
QKV_TM = 1024
QKV_TN = 512


def _qkv_proj_kernel(x_ref, g_ref, wq_ref, wk_ref, wv_ref, cos_ref, sin_ref, q_ref, k_ref, v_ref, xn_ref,
                     *scratch, d):
    j = pl.program_id(2)
    tm, dm = x_ref.shape[1], x_ref.shape[2]
    nt = tm // d

    def chain_rows(ref, r):
        return ref[pl.ds(r, nt, stride=d), :]

    @pl.when(j == 0)
    def _():
        xnorm = _rms_rows(x_ref[0], g_ref[...], NORM_EPS)
        if d == 1:
            xn_ref[...] = xnorm.astype(BF16)
        else:
            xs_ref = scratch[0]
            for c in range(dm // LANES):
                xs_ref[c] = xnorm[:, c * LANES:(c + 1) * LANES]
            for r in range(d):
                for c in range(dm // LANES):
                    xn_ref[r * nt:(r + 1) * nt, c * LANES:(c + 1) * LANES] = chain_rows(xs_ref.at[c], r).astype(BF16)

    if d == 1:
        ct, st = cos_ref[...], sin_ref[...]
    else:
        ct = jnp.concatenate([chain_rows(cos_ref, r) for r in range(d)], axis=0)
        st = jnp.concatenate([chain_rows(sin_ref, r) for r in range(d)], axis=0)
    lane = lax.broadcasted_iota(jnp.int32, (1, LANES), 1)
    first_half = (lane % HEAD_DIM) < (HEAD_DIM // 2)
    xn = xn_ref[...]
    parts = ((wq_ref, q_ref), (wk_ref, k_ref), (wv_ref, v_ref))
    matmul = lambda part: jnp.dot(xn, parts[part][0][...], preferred_element_type=F32)
    pending = matmul(0)
    for part, (w_ref, o_ref) in enumerate(parts):
        y = pending
        pending = matmul(part + 1) if part + 1 < len(parts) else None
        for c in range(w_ref.shape[1] // LANES):
            yc = y[:, c * LANES:(c + 1) * LANES]
            if part < 2:
                partner = jnp.where(first_half, pltpu.roll(yc, LANES - HEAD_DIM // 2, 1),
                                    pltpu.roll(yc, HEAD_DIM // 2, 1))
                yc = yc * ct + partner * st
            if part == 0:
                yc = yc * Q_SCALE
            for r in range(d):
                o_ref[0, c, r] = yc[r * nt:(r + 1) * nt, :].astype(BF16)


def _qkv_proj(h, gain, w, col0, cos, sin, dilation):
    b, s, dm = h.shape
    d = dilation
    n = s // d
    tm, tn = QKV_TM, QKV_TN
    nt = tm // d
    assert s % tm == 0 and dm % tn == 0 and col0 % tn == 0 and nt % 16 == 0
    c0 = col0 // tn
    per = dm // tn
    out = jax.ShapeDtypeStruct((b, dm // LANES, d, n, LANES), BF16)
    out_spec = pl.BlockSpec((1, tn // LANES, d, nt, LANES), lambda bi, i, j: (bi, j, 0, i, 0))
    scratch = [pltpu.VMEM((tm, dm), BF16)]
    if d > 1:
        scratch.append(pltpu.VMEM((dm // LANES, tm, LANES), F32))
    return pl.pallas_call(
        functools.partial(_qkv_proj_kernel, d=d),
        grid=(b, s // tm, per),
        in_specs=[
            pl.BlockSpec((1, tm, dm), lambda bi, i, j: (bi, i, 0)),
            pl.BlockSpec((1, dm), lambda bi, i, j: (0, 0)),
            pl.BlockSpec((dm, tn), lambda bi, i, j: (0, c0 + j)),
            pl.BlockSpec((dm, tn), lambda bi, i, j: (0, c0 + per + j)),
            pl.BlockSpec((dm, tn), lambda bi, i, j: (0, c0 + 2 * per + j)),
            pl.BlockSpec((tm, LANES), lambda bi, i, j: (i, 0)),
            pl.BlockSpec((tm, LANES), lambda bi, i, j: (i, 0)),
        ],
        out_specs=(out_spec, out_spec, out_spec),
        out_shape=(out, out, out),
        scratch_shapes=scratch,
        compiler_params=_params("parallel", "parallel", "arbitrary"),
        name=f"qkv_proj_d{d}",
    )(h, gain.reshape(1, dm), w, w, w, cos, sin)


def _out_proj_hm_kernel(a_ref, w_ref, h_ref, o_ref):
    a = jnp.concatenate([a_ref[0, p] for p in range(a_ref.shape[1])], axis=-1)
    o_ref[0] = h_ref[0] + jnp.dot(a, w_ref[...], preferred_element_type=F32)


def _out_proj_hm(a_hm, w, h):
    b, s, dm = h.shape
    kb = a_hm.shape[1]
    tm = 1024
    return pl.pallas_call(
        _out_proj_hm_kernel,
        grid=(b, s // tm),
        in_specs=[
            pl.BlockSpec((1, kb, tm, LANES), lambda bi, i: (bi, 0, i, 0)),
            pl.BlockSpec(w.shape, lambda bi, i: (0, 0)),
            pl.BlockSpec((1, tm, dm), lambda bi, i: (bi, i, 0)),
        ],
        out_specs=pl.BlockSpec((1, tm, dm), lambda bi, i: (bi, i, 0)),
        out_shape=jax.ShapeDtypeStruct(h.shape, F32),
        compiler_params=_params("parallel", "parallel"),
        name="out_proj_hm",
    )(a_hm, w, h)


def _out_proj_kernel(a_ref, w_ref, h_ref, o_ref):
    o_ref[0] = h_ref[0] + jnp.dot(a_ref[0], w_ref[...], preferred_element_type=F32)


def _out_proj(a, w, h):
    b, s, dm = h.shape
    k = a.shape[-1]
    tm = 1024
    return pl.pallas_call(
        _out_proj_kernel,
        grid=(b, s // tm),
        in_specs=[
            pl.BlockSpec((1, tm, k), lambda bi, i: (bi, i, 0)),
            pl.BlockSpec(w.shape, lambda bi, i: (0, 0)),
            pl.BlockSpec((1, tm, dm), lambda bi, i: (bi, i, 0)),
        ],
        out_specs=pl.BlockSpec((1, tm, dm), lambda bi, i: (bi, i, 0)),
        out_shape=jax.ShapeDtypeStruct(h.shape, F32),
        compiler_params=_params("parallel", "parallel"),
        name="out_proj",
    )(a, w, h)


POOL_PAD = 8
POOL_ROWS = 256


def _pool_kernel(h_ref, g_ref, w_ref, sc_ref, o_ref, hn_ref):
    s, dm = h_ref.shape[1], h_ref.shape[2]
    group = dm // len(POOL_WINDOWS)
    zeros = jnp.zeros((POOL_PAD, dm), F32)
    hn_ref[0:POOL_PAD, :] = zeros
    hn_ref[POOL_PAD + s:POOL_PAD + s + POOL_PAD, :] = zeros

    def norm_step(i, carry):
        r0 = pl.multiple_of(i * POOL_ROWS, POOL_ROWS)
        hn_ref[pl.ds(POOL_PAD + r0, POOL_ROWS), :] = _rms_rows(h_ref[0, pl.ds(r0, POOL_ROWS), :], g_ref[...], NORM_EPS)
        return carry

    lax.fori_loop(0, s // POOL_ROWS, norm_step, 0)

    def mix_step(i, carry):
        r0 = pl.multiple_of(i * POOL_ROWS, POOL_ROWS)
        t = r0 + lax.broadcasted_iota(jnp.int32, (POOL_ROWS, 1), 0)
        inner = slice(SUBLANES, SUBLANES + POOL_ROWS)
        for g, win in enumerate(POOL_WINDOWS):
            half = win // 2
            sl = slice(g * group, (g + 1) * group)
            xw = _halo_window(hn_ref, POOL_PAD + r0, POOL_ROWS, sl)
            tot = xw + _shift_rows(xw, -1)
            step = 1
            while 2 * step < win:
                tot = _shift_rows(tot, -step) + _shift_rows(tot, step)
                step *= 2
            cnt = (jnp.minimum(t + half, s) - jnp.maximum(t - half, 0)).astype(F32)
            pooled = (tot[inner] / cnt - xw[inner]).astype(BF16)
            y = jnp.dot(pooled, w_ref[g], preferred_element_type=F32) * sc_ref[:, sl]
            o_ref[0, pl.ds(r0, POOL_ROWS), sl] = h_ref[0, pl.ds(r0, POOL_ROWS), sl] + y
        return carry

    lax.fori_loop(0, s // POOL_ROWS, mix_step, 0)


def _pool_mixer(h, gain, w, scale):
    b, s, dm = h.shape
    return pl.pallas_call(
        _pool_kernel,
        grid=(b,),
        in_specs=[
            pl.BlockSpec((1, s, dm), lambda bi: (bi, 0, 0)),
            pl.BlockSpec((1, dm), lambda bi: (0, 0)),
            pl.BlockSpec(w.shape, lambda bi: (0, 0, 0)),
            pl.BlockSpec((1, dm), lambda bi: (0, 0)),
        ],
        out_specs=pl.BlockSpec((1, s, dm), lambda bi: (bi, 0, 0)),
        out_shape=jax.ShapeDtypeStruct(h.shape, F32),
        scratch_shapes=[pltpu.VMEM((s + 2 * POOL_PAD, dm), F32)],
        compiler_params=_params("parallel"),
        name="pool_mixer",
    )(h, gain.reshape(1, dm), w.astype(BF16), scale.reshape(1, dm))


DIFF_TQ = 256
DIFF_CHAINS = 4
DIFF_AHEAD = 2


def _diff_attn_kernel(q_ref, k_ref, v_ref, lam_ref, sub_ref, o_ref, *, lam_init):
    k = k_ref[0, 0, 0]
    v = v_ref[0, 0, 0]
    lp = lam_ref[...]
    lam = (jnp.exp(jnp.sum(lp[0:1] * lp[1:2], axis=-1, keepdims=True))
           - jnp.exp(jnp.sum(lp[2:3] * lp[3:4], axis=-1, keepdims=True)) + lam_init)
    lane = lax.broadcasted_iota(jnp.int32, (1, LANES), 1)
    first = lane < HEAD_DIM

    def scores(c):
        q = q_ref[0, 0, 0, c * DIFF_TQ:(c + 1) * DIFF_TQ, :]
        zero = jnp.zeros_like(q)
        dn = (((1,), (1,)), ((), ()))
        return (lax.dot_general(jnp.where(first, q, zero), k, dn, preferred_element_type=F32),
                lax.dot_general(jnp.where(first, zero, q), k, dn, preferred_element_type=F32))

    def softmax_parts(sc):
        p = jnp.exp2(sc - jnp.max(sc, axis=-1, keepdims=True))
        return p.astype(BF16), jnp.sum(p, axis=-1, keepdims=True)

    pending = [scores(c) for c in range(min(DIFF_AHEAD, DIFF_CHAINS))]
    for c in range(DIFF_CHAINS):
        if c + DIFF_AHEAD < DIFF_CHAINS:
            pending.append(scores(c + DIFF_AHEAD))
        sc = pending.pop(0)
        p0, l0 = softmax_parts(sc[0])
        p1, l1 = softmax_parts(sc[1])
        a = p0 * (1.0 / l0).astype(BF16) - p1 * (lam / l1).astype(BF16)
        o = jnp.dot(a, v, preferred_element_type=F32)
        o = _rms_rows(o, sub_ref[...], DIFF_SUBLN_EPS) * (1.0 - lam_init)
        o_ref[0, 0, c * DIFF_TQ:(c + 1) * DIFF_TQ, :] = o.astype(BF16)


def _diff_attention(q, k, v, lam_params, subln, lam_init):
    b, heads, _, s, _ = q.shape
    tq = DIFF_TQ * DIFF_CHAINS
    return pl.pallas_call(
        functools.partial(_diff_attn_kernel, lam_init=lam_init),
        grid=(b, heads, s // tq),
        in_specs=[
            pl.BlockSpec((1, 1, 1, tq, LANES), lambda bi, h, i: (bi, h, 0, i, 0)),
            pl.BlockSpec((1, 1, 1, s, LANES), lambda bi, h, i: (bi, h, 0, 0, 0)),
            pl.BlockSpec((1, 1, 1, s, LANES), lambda bi, h, i: (bi, h, 0, 0, 0)),
            pl.BlockSpec(lam_params.shape, lambda bi, h, i: (0, 0)),
            pl.BlockSpec((1, LANES), lambda bi, h, i: (0, 0)),
        ],
        out_specs=pl.BlockSpec((1, 1, tq, LANES), lambda bi, h, i: (bi, h, i, 0)),
        out_shape=jax.ShapeDtypeStruct((b, heads, s, LANES), BF16),
        compiler_params=_params("parallel", "parallel", "parallel"),
        name="diff_attention",
    )(q, k, v, lam_params, subln.reshape(1, LANES))


LRU_PAD = 8
LRU_ROWS = 256
LRU_PROJ_ROWS = 1024
LRU_CW = 256
LRU_SCAN_TILES = 8


def _log_sigmoid(x):
    return jnp.minimum(x, 0.0) - jnp.log1p(jnp.exp(-jnp.abs(x)))


def _gelu_tanh(x):
    return 0.5 * x * (1.0 + jnp.tanh(math.sqrt(2.0 / math.pi) * (x + 0.044715 * (x * x * x))))


def _lru_kernel(h_ref, g_ref, wg_ref, wu_ref, cw_ref, cb_ref, wbd_ref, bias_ref, lam_ref, o_ref,
                xn_ref, gate_ref, upad_ref, a_ref, b_ref, c_ref):
    s = h_ref.shape[1]
    cw = wg_ref.shape[1]
    n_slab = cw // LANES
    assert s == SUBLANES * LRU_ROWS

    @pl.when(pl.program_id(1) == 0)
    def _():
        def norm_step(i, carry):
            r0 = pl.multiple_of(i * LRU_ROWS, LRU_ROWS)
            xn_ref[pl.ds(r0, LRU_ROWS), :] = _rms_rows(h_ref[0, pl.ds(r0, LRU_ROWS), :], g_ref[...],
                                                       NORM_EPS).astype(BF16)
            return carry

        lax.fori_loop(0, s // LRU_ROWS, norm_step, 0)

    zeros = jnp.zeros((LRU_PAD, cw), F32)
    upad_ref[0:LRU_PAD, :] = zeros
    upad_ref[LRU_PAD + s:LRU_PAD + s + LRU_PAD, :] = zeros
    wgu = jnp.concatenate([wg_ref[...], wu_ref[...]], axis=-1)
    for blk in range(s // LRU_PROJ_ROWS):
        rows = slice(blk * LRU_PROJ_ROWS, (blk + 1) * LRU_PROJ_ROWS)
        gu = jnp.dot(xn_ref[rows, :], wgu, preferred_element_type=F32)
        gate_ref[rows, :] = gu[:, :cw]
        upad_ref[LRU_PAD + blk * LRU_PROJ_ROWS:LRU_PAD + (blk + 1) * LRU_PROJ_ROWS, :] = gu[:, cw:]

    row8 = lax.broadcasted_iota(jnp.int32, (SUBLANES, 1), 0)
    for dr in range(2):
        half_unit = (0.5 * LRU_C) * _log_sigmoid(lam_ref[dr:dr + 1, :])

        def gate_step(i, carry, dr=dr, half_unit=half_unit):
            r0 = pl.multiple_of(i * LRU_ROWS, LRU_ROWS)
            uw = _halo_window(upad_ref, LRU_PAD + r0, LRU_ROWS)
            xc = cb_ref[dr:dr + 1, :]
            for j in range(LRU_CONV):
                off = (j - (LRU_CONV - 1)) if dr == 0 else ((LRU_CONV - 1) - j)
                xc = xc + cw_ref[dr, j:j + 1, :] * _shift_rows(uw, off)[SUBLANES:SUBLANES + LRU_ROWS]
            pre = jnp.dot(xc.astype(BF16), wbd_ref[dr, 0], preferred_element_type=F32) + bias_ref[0, dr:dr + 1, :]
            t_r = jnp.tanh(pre[:, :cw])
            t_i = jnp.tanh(pre[:, cw:])
            log_a = t_r * half_unit + half_unit
            a = jnp.exp(log_a)
            th = jnp.tanh(log_a)
            bv = jnp.sqrt(-0.5 * th / (1.0 - th)) * ((t_i + 1.0) * xc)
            for sl in range(n_slab):
                lanes = slice(sl * LANES, (sl + 1) * LANES)
                a_ref[dr, sl, pl.ds(i, LRU_ROWS, stride=SUBLANES), :] = a[:, lanes]
                b_ref[dr, sl, pl.ds(i, LRU_ROWS, stride=SUBLANES), :] = bv[:, lanes]
            return carry

        lax.fori_loop(0, SUBLANES, gate_step, 0)

        n_blocks = LRU_ROWS // LRU_SCAN_TILES

        def scan_block(tb, carry, dr=dr):
            blk = tb if dr == 0 else n_blocks - 1 - tb
            base = pl.multiple_of(blk * (LRU_SCAN_TILES * SUBLANES), LRU_SCAN_TILES * SUBLANES)
            hs, ps = list(carry[0]), list(carry[1])
            for u in range(LRU_SCAN_TILES):
                tile = u if dr == 0 else LRU_SCAN_TILES - 1 - u
                rows = pl.ds(base + tile * SUBLANES, SUBLANES)
                for sl in range(n_slab):
                    a_t = a_ref[dr, sl, rows, :]
                    hs[sl] = a_t * hs[sl] + b_ref[dr, sl, rows, :]
                    ps[sl] = a_t * ps[sl]
                    b_ref[dr, sl, rows, :] = hs[sl]
                    a_ref[dr, sl, rows, :] = ps[sl]
            return tuple(hs), tuple(ps)

        init = (tuple(jnp.zeros((SUBLANES, LANES), F32) for _ in range(n_slab)),
                tuple(jnp.ones((SUBLANES, LANES), F32) for _ in range(n_slab)))
        h_end, p_end = lax.fori_loop(0, n_blocks, scan_block, init)

        for sl in range(n_slab):
            c = jnp.zeros((SUBLANES, LANES), F32)
            for _ in range(SUBLANES - 1):
                nxt = h_end[sl] + p_end[sl] * c
                if dr == 0:
                    c = jnp.where(row8 == 0, 0.0, _shift_rows(nxt, -1))
                else:
                    c = jnp.where(row8 == SUBLANES - 1, 0.0, _shift_rows(nxt, 1))
            c_ref[dr, sl] = c

    def fix_step(t, carry):
        r0 = pl.multiple_of(t * LRU_ROWS, LRU_ROWS)
        for sl in range(n_slab):
            tot = None
            for dr in range(2):
                local = b_ref[dr, sl, pl.ds(r0, LRU_ROWS), :].reshape(-1, SUBLANES, LANES)
                decay = a_ref[dr, sl, pl.ds(r0, LRU_ROWS), :].reshape(-1, SUBLANES, LANES)
                fixed = local + decay * c_ref[dr, sl]
                tot = fixed if tot is None else tot + fixed
            b_ref[0, sl, pl.ds(r0, LRU_ROWS), :] = tot.reshape(LRU_ROWS, LANES)
        return carry

    lax.fori_loop(0, SUBLANES, fix_step, 0)

    for i in range(SUBLANES):
        rows = slice(i * LRU_ROWS, (i + 1) * LRU_ROWS)
        hsum = jnp.concatenate([b_ref[0, sl, pl.ds(i, LRU_ROWS, stride=SUBLANES), :] for sl in range(n_slab)],
                               axis=-1)
        o_ref[0, rows, :] = (_gelu_tanh(gate_ref[rows, :]) * hsum).astype(BF16)


def _lru_core(h, gain, w_in, conv_w, conv_b, w_a, b_a, w_x, b_x, lam):
    b, s, dm = h.shape
    c = w_in.shape[1] // 2
    cw = LRU_CW
    ncb = c // cw
    per = cw // HEAD_DIM
    eye = jnp.eye(per, dtype=F32)

    def block_diag(w):
        w5 = w.reshape(2, ncb, per, HEAD_DIM, HEAD_DIM)
        return jnp.einsum('dcipq,ik->dcipkq', w5, eye).reshape(2, ncb, cw, cw)

    wbd = (0.5 * jnp.concatenate([block_diag(w_a), block_diag(w_x)], axis=-1)).astype(BF16)
    bias = 0.5 * jnp.concatenate([b_a.reshape(2, ncb, cw), b_x.reshape(2, ncb, cw)], axis=-1)
    bias = bias.transpose(1, 0, 2)
    n_slab = cw // LANES
    return pl.pallas_call(
        _lru_kernel,
        grid=(b, ncb),
        in_specs=[
            pl.BlockSpec((1, s, dm), lambda bi, cb: (bi, 0, 0), pipeline_mode=pl.Buffered(1)),
            pl.BlockSpec((1, dm), lambda bi, cb: (0, 0)),
            pl.BlockSpec((dm, cw), lambda bi, cb: (0, cb)),
            pl.BlockSpec((dm, cw), lambda bi, cb: (0, ncb + cb)),
            pl.BlockSpec((2, LRU_CONV, cw), lambda bi, cb: (0, 0, cb)),
            pl.BlockSpec((2, cw), lambda bi, cb: (0, cb)),
            pl.BlockSpec((2, 1, cw, 2 * cw), lambda bi, cb: (0, cb, 0, 0)),
            pl.BlockSpec((1, 2, 2 * cw), lambda bi, cb: (cb, 0, 0)),
            pl.BlockSpec((2, cw), lambda bi, cb: (0, cb)),
        ],
        out_specs=pl.BlockSpec((1, s, cw), lambda bi, cb: (bi, 0, cb)),
        out_shape=jax.ShapeDtypeStruct((b, s, c), BF16),
        scratch_shapes=[
            pltpu.VMEM((s, dm), BF16),
            pltpu.VMEM((s, cw), F32),
            pltpu.VMEM((s + 2 * LRU_PAD, cw), F32),
            pltpu.VMEM((2, n_slab, s, LANES), F32),
            pltpu.VMEM((2, n_slab, s, LANES), F32),
            pltpu.VMEM((2, n_slab, SUBLANES, LANES), F32),
        ],
        compiler_params=_params("parallel", "arbitrary"),
        name="lru_core",
    )(h, gain.reshape(1, dm), w_in, w_in, conv_w, conv_b, wbd, bias, lam)


DIL_TQ = 128
DIL_UNROLL = 4


def _band_bias(tq, w, rel0, half):
    qi = lax.broadcasted_iota(jnp.int32, (tq, 1), 0)
    ki = lax.broadcasted_iota(jnp.int32, (1, w), 1)
    return jnp.where(jnp.abs(ki + rel0 - qi) <= half, 0.0, NEG_INF).astype(F32)


def _band_scores(q, k):
    lane = lax.broadcasted_iota(jnp.int32, (1, LANES), 1)
    first = lane < HEAD_DIM
    zero = jnp.zeros_like(q)
    qq = jnp.concatenate([jnp.where(first, q, zero), jnp.where(first, zero, q)], axis=0)
    return lax.dot_general(qq, k, (((1,), (1,)), ((), ())), preferred_element_type=F32)


def _band_softmax_pv(sc, v, bias):
    tq = sc.shape[0] // 2
    lane = lax.broadcasted_iota(jnp.int32, (1, LANES), 1)
    first = lane < HEAD_DIM
    sc = sc + jnp.concatenate([bias, bias], axis=0)
    m = jnp.max(sc, axis=-1, keepdims=True)
    p = jnp.exp2(sc - m).astype(BF16)
    v1 = jnp.concatenate([v, jnp.ones_like(v)], axis=-1)
    pv = jnp.dot(p, v1, preferred_element_type=F32)
    m_b = jnp.broadcast_to(m, (2 * tq, LANES))
    pick = lambda t: jnp.where(first, t[:tq], t[tq:])
    return pick(m_b), pick(pv[:, LANES:]), pick(pv[:, :LANES])


def _dilated_attn_kernel(q0_ref, k0_ref, v0_ref, q1_ref, k1_ref, v1_ref, q2_ref, k2_ref, v2_ref,
                         o_ref, m_ref, l_ref, acc_ref, bias_ref, *, halves, dils):
    s = o_ref.shape[2]
    tq = DIL_TQ
    win = 2 * tq

    def merge(rows, m_new, l_new, acc_new):
        m_old, l_old, acc_old = m_ref[rows, :], l_ref[rows, :], acc_ref[rows, :]
        m_tot = jnp.maximum(m_old, m_new)
        alpha = jnp.exp2(m_old - m_tot)
        beta = jnp.exp2(m_new - m_tot)
        m_ref[rows, :] = m_tot
        l_ref[rows, :] = alpha * l_old + beta * l_new
        acc_ref[rows, :] = alpha * acc_old + beta * acc_new

    groups = ((q0_ref, k0_ref, v0_ref), (q1_ref, k1_ref, v1_ref), (q2_ref, k2_ref, v2_ref))
    order = sorted(range(len(groups)), key=lambda gi: -dils[gi])
    for g in order:
        q_ref, k_ref, v_ref = groups[g]
        half, d = halves[g], dils[g]
        n = s // d
        tiles = n // tq
        w = min(win, n)
        for case, rel0 in enumerate((0, -(tq // 2), tq - w) if tiles > 1 else (0,)):
            bias_ref[case, :, 0:w] = _band_bias(tq, w, rel0, half)

        def batch_step(cb, carry, g=g, q_ref=q_ref, k_ref=k_ref, v_ref=v_ref, d=d, n=n, tiles=tiles, w=w):
            todo = []
            for u in range(DIL_UNROLL):
                c = cb * DIL_UNROLL + u
                r = c // tiles
                i = c % tiles
                q0 = pl.multiple_of(i * tq, tq)
                ks = pl.multiple_of(jnp.clip(q0 - tq // 2, 0, n - w), tq // 2)
                case = jnp.where(i == 0, 0, jnp.where(i == tiles - 1, 2, 1)) if tiles > 1 else 0
                sc = _band_scores(q_ref[0, 0, r, pl.ds(q0, tq), :], k_ref[0, 0, r, pl.ds(ks, w), :])
                todo.append((sc, r, q0, ks, case))
            for sc, r, q0, ks, case in todo:
                m_new, l_new, acc_new = _band_softmax_pv(sc, v_ref[0, 0, r, pl.ds(ks, w), :],
                                                         bias_ref[case, :, 0:w])
                rows = pl.ds(q0, tq) if d == 1 else pl.ds(q0 * d + r, tq, stride=d)
                if g == order[0]:
                    m_ref[rows, :] = m_new
                    l_ref[rows, :] = l_new
                    acc_ref[rows, :] = acc_new
                else:
                    merge(rows, m_new, l_new, acc_new)
            return carry

        assert (d * tiles) % DIL_UNROLL == 0
        lax.fori_loop(0, d * tiles // DIL_UNROLL, batch_step, 0)

    def out_step(i, carry):
        r0 = pl.multiple_of(i * 256, 256)
        o_ref[0, 0, pl.ds(r0, 256), :] = (acc_ref[pl.ds(r0, 256), :] / l_ref[pl.ds(r0, 256), :]).astype(BF16)
        return carry

    lax.fori_loop(0, s // 256, out_step, 0)


def _dilated_attention(qkv_groups):
    b, pairs, d0, s, _ = qkv_groups[0][0].shape
    assert d0 == 1
    halves = tuple(w // (2 * d) for (w, d) in DIL_GROUPS)
    dils = tuple(d for (_, d) in DIL_GROUPS)
    in_specs, args = [], []
    for q, k, v in qkv_groups:
        d, n = q.shape[2], q.shape[3]
        for arr in (q, k, v):
            in_specs.append(pl.BlockSpec((1, 1, d, n, LANES), lambda bi, p: (bi, p, 0, 0, 0)))
            args.append(arr)
    return pl.pallas_call(
        functools.partial(_dilated_attn_kernel, halves=halves, dils=dils),
        grid=(b, pairs),
        in_specs=in_specs,
        out_specs=pl.BlockSpec((1, 1, s, LANES), lambda bi, p: (bi, p, 0, 0)),
        out_shape=jax.ShapeDtypeStruct((b, pairs, s, LANES), BF16),
        scratch_shapes=[pltpu.VMEM((s, LANES), F32), pltpu.VMEM((s, LANES), F32), pltpu.VMEM((s, LANES), F32),
                        pltpu.VMEM((3, DIL_TQ, 2 * DIL_TQ), F32)],
        compiler_params=_params("parallel", "parallel"),
        name="dilated_attention",
    )(*args)


FFN_HALO = 16
FFN_UP_ROWS = 1024
FFN_DOWN_ROWS = 512
FFN_CHUNK = 256
FFN_STEP_CHUNKS = 4


def _ffn_kernel(h_ref, g_ref, wg_ref, *rest, final_norm, n_chunks):
    wu_refs = rest[:FFN_STEP_CHUNKS]
    cw_ref, cb_ref, wd_ref, hr_ref, fg_ref, o_ref, xn_ref, hid_ref = rest[FFN_STEP_CHUNKS:]
    j = pl.program_id(1)
    s = h_ref.shape[1]
    n_full = n_chunks // FFN_STEP_CHUNKS
    n_up = -(-n_chunks // FFN_STEP_CHUNKS)

    @pl.when(j == 0)
    def _():
        zeros = jnp.zeros((FFN_HALO, xn_ref.shape[1]), BF16)
        xn_ref[0:FFN_HALO, :] = zeros
        xn_ref[FFN_HALO + s:FFN_HALO + s + FFN_HALO, :] = zeros

        def norm_step(i, carry):
            r0 = pl.multiple_of(i * FFN_DOWN_ROWS, FFN_DOWN_ROWS)
            x = h_ref[0, pl.ds(r0, FFN_DOWN_ROWS), :]
            xn_ref[pl.ds(FFN_HALO + r0, FFN_DOWN_ROWS), :] = _rms_rows(x, g_ref[...], NORM_EPS).astype(BF16)
            return carry

        lax.fori_loop(0, s // FFN_DOWN_ROWS, norm_step, 0)

    def up_chunks(n_here):
        inner = slice(FFN_HALO, FFN_HALO + FFN_UP_ROWS)
        jobs = []
        for k in range(n_here):
            c = j * FFN_STEP_CHUNKS + k
            cols = slice(k * FFN_CHUNK, (k + 1) * FFN_CHUNK)
            wgu = jnp.concatenate([wg_ref[:, cols], wu_refs[k][...]], axis=-1)
            jobs += [(c, cols, wgu, rc) for rc in range(s // FFN_UP_ROWS)]

        def matmul(job):
            _, _, wgu, rc = job
            xs = xn_ref[rc * FFN_UP_ROWS:rc * FFN_UP_ROWS + FFN_UP_ROWS + 2 * FFN_HALO, :]
            return jnp.dot(xs, wgu, preferred_element_type=F32)

        def epilogue(job, gu):
            c, cols, _, rc = job
            g = gu[:, :FFN_CHUNK]
            gc = cb_ref[:, cols]
            for t in range(FFN_CONV):
                gc = gc + cw_ref[t:t + 1, cols] * _shift_rows(g, t - FFN_CONV // 2)[inner]
            act = 0.5 * gc * (1.0 + lax.erf(gc * (1.0 / math.sqrt(2.0))))
            hid_ref[c, rc * FFN_UP_ROWS:(rc + 1) * FFN_UP_ROWS, :] = (act * gu[inner, FFN_CHUNK:]).astype(BF16)

        pending = matmul(jobs[0])
        for n, job in enumerate(jobs):
            nxt = matmul(jobs[n + 1]) if n + 1 < len(jobs) else None
            epilogue(job, pending)
            pending = nxt

    @pl.when(j < n_full)
    def _():
        up_chunks(FFN_STEP_CHUNKS)

    if n_chunks % FFN_STEP_CHUNKS:
        @pl.when(j == n_full)
        def _():
            up_chunks(n_chunks % FFN_STEP_CHUNKS)

    @pl.when(j >= n_up)
    def _():
        r0 = pl.multiple_of((j - n_up) * FFN_DOWN_ROWS, FFN_DOWN_ROWS)
        hid = jnp.concatenate([hid_ref[c, pl.ds(r0, FFN_DOWN_ROWS), :] for c in range(n_chunks)], axis=-1)
        out = hr_ref[0] + jnp.dot(hid, wd_ref[...], preferred_element_type=F32)
        if final_norm:
            out = _rms_rows(out, fg_ref[...], NORM_EPS)
        o_ref[0] = out


def _conv_ffn(h, gain, w_up, conv_w, conv_b, w_down, final_gain):
    b, s, dm = h.shape
    dff = w_down.shape[0]
    n_chunks = dff // FFN_CHUNK
    per = FFN_STEP_CHUNKS
    n_up = -(-n_chunks // per)
    n_down = s // FFN_DOWN_ROWS
    assert dff % FFN_CHUNK == 0 and s % FFN_UP_ROWS == 0 and s % FFN_DOWN_ROWS == 0
    final_norm = final_gain is not None
    fg = (final_gain if final_norm else gain).reshape(1, dm)
    pad = per * n_up * FFN_CHUNK - dff
    conv_w = jnp.pad(conv_w, ((0, 0), (0, pad)))
    conv_b = jnp.pad(conv_b.reshape(1, dff), ((0, 0), (0, pad)))
    up_idx = lambda j: jnp.minimum(j, n_up - 1)
    down_idx = lambda j: jnp.maximum(j - n_up, 0)
    value_specs = [pl.BlockSpec((dm, FFN_CHUNK), lambda bi, j, k=k: (0, jnp.minimum(n_chunks + per * up_idx(j) + k,
                                                                                   2 * n_chunks - 1)))
                   for k in range(per)]
    return pl.pallas_call(
        functools.partial(_ffn_kernel, final_norm=final_norm, n_chunks=n_chunks),
        grid=(b, n_up + n_down),
        in_specs=[
            pl.BlockSpec((1, s, dm), lambda bi, j: (bi, 0, 0)),
            pl.BlockSpec((1, dm), lambda bi, j: (0, 0)),
            pl.BlockSpec((dm, per * FFN_CHUNK), lambda bi, j: (0, up_idx(j))),
            *value_specs,
            pl.BlockSpec((FFN_CONV, per * FFN_CHUNK), lambda bi, j: (0, up_idx(j))),
            pl.BlockSpec((1, per * FFN_CHUNK), lambda bi, j: (0, up_idx(j))),
            pl.BlockSpec((dff, dm), lambda bi, j: (0, 0), pipeline_mode=pl.Buffered(1)),
            pl.BlockSpec((1, FFN_DOWN_ROWS, dm), lambda bi, j: (bi, down_idx(j), 0)),
            pl.BlockSpec((1, dm), lambda bi, j: (0, 0)),
        ],
        out_specs=pl.BlockSpec((1, FFN_DOWN_ROWS, dm), lambda bi, j: (bi, down_idx(j), 0)),
        out_shape=jax.ShapeDtypeStruct(h.shape, F32),
        scratch_shapes=[pltpu.VMEM((s + 2 * FFN_HALO, dm), BF16), pltpu.VMEM((n_chunks, s, FFN_CHUNK), BF16)],
        compiler_params=_params("parallel", "arbitrary"),
        name="conv_ffn",
    )(h, gain.reshape(1, dm), w_up, *([w_up] * per), conv_w, conv_b, w_down, h, fg)


def kernel(x, positions, mix_norm, pool_w, pool_scale, diff_w_qkv, diff_lam_q1, diff_lam_k1, diff_lam_q2,
           diff_lam_k2, diff_subln, diff_w_o, lru_w_in, lru_conv_w, lru_conv_b, lru_w_a, lru_b_a, lru_w_x,
           lru_b_x, lru_lambda, lru_w_out, dil_w_qkv, dil_w_o, ffn_norm, ffn_w_up, ffn_conv_w, ffn_conv_b,
           ffn_w_down, final_norm):
    b, s, dm = x.shape
    depth = mix_norm.shape[0]
    cos, sin = _rope_tables(positions)
    h = x
    for i in range(depth):
        m, j = i % N_MIXERS, i // N_MIXERS
        if m == 0:
            h = _pool_mixer(h, mix_norm[i], pool_w[j], pool_scale[j])
        elif m == 1:
            q, k, v = _qkv_proj(h, mix_norm[i], diff_w_qkv[j].astype(BF16), 0, cos, sin, 1)
            lam_params = jnp.zeros((SUBLANES, LANES), F32)
            lam_params = lam_params.at[0:4, 0:HEAD_DIM].set(
                jnp.stack([diff_lam_q1[j], diff_lam_k1[j], diff_lam_q2[j], diff_lam_k2[j]]).astype(F32))
            lam_init = 0.8 - 0.6 * math.exp(-0.3 * i)
            o = _diff_attention(q, k, v, lam_params, diff_subln[j], lam_init)
            h = _out_proj_hm(o, diff_w_o[j].astype(BF16), h)
        elif m == 2:
            y = _lru_core(h, mix_norm[i], lru_w_in[j].astype(BF16), lru_conv_w[j], lru_conv_b[j], lru_w_a[j],
                          lru_b_a[j], lru_w_x[j], lru_b_x[j], lru_lambda[j])
            h = _out_proj(y, lru_w_out[j].astype(BF16), h)
        else:
            w = dil_w_qkv[j].astype(BF16)
            groups = [_qkv_proj(h, mix_norm[i], w, g * 3 * dm, cos, sin, d)
                      for g, (_, d) in enumerate(DIL_GROUPS)]
            o = _dilated_attention(groups)
            h = _out_proj_hm(o, dil_w_o[j].astype(BF16), h)
        h = _conv_ffn(h, ffn_norm[i], ffn_w_up[i].astype(BF16), ffn_conv_w[i], ffn_conv_b[i],
                      ffn_w_down[i].astype(BF16), final_norm if i == depth - 1 else None)
    return h
```

```python
import functools
import math

import jax
import jax.numpy as jnp
from jax import lax
from jax.experimental import pallas as pl
from jax.experimental.pallas import tpu as pltpu

F32 = jnp.float32
BF16 = jnp.bfloat16

HEAD_DIM = 64
NORM_EPS = 1e-6
ROPE_THETA = 10000.0
NEG_INF = -1e30
POOL_WINDOWS = (2, 4, 8, 16)
DIFF_SUBLN_EPS = 1e-5
LRU_C = 8.0
LRU_CONV = 4
DIL_GROUPS = ((128, 1), (512, 4), (2048, 16))
FFN_CONV = 3
N_MIXERS = 4

LANES = 128
SUBLANES = 8
VMEM_LIMIT_BYTES = 56 * 1024 * 1024


def _params(*semantics):
    return pltpu.CompilerParams(dimension_semantics=semantics, vmem_limit_bytes=VMEM_LIMIT_BYTES)


def _rms_rows(x, gain, eps):
    ms = jnp.mean(x * x, axis=-1, keepdims=True)
    return x * lax.rsqrt(ms + eps) * gain


def _halo_window(ref, r0, rows, cols=slice(None)):
    return ref[pl.ds(r0 - SUBLANES, rows + 2 * SUBLANES), cols]


def _shift_rows(win, off):
    if off == 0:
        return win
    return pltpu.roll(win, (-off) % win.shape[0], 0)


def _rope_table_kernel(pos_ref, inv_ref, cos_ref, sin_ref):
    ang = pos_ref[...] * inv_ref[...]
    lane = lax.broadcasted_iota(jnp.int32, ang.shape, 1)
    first_half = (lane % HEAD_DIM) < (HEAD_DIM // 2)
    cos_ref[...] = jnp.cos(ang)
    sin_ref[...] = jnp.where(first_half, -jnp.sin(ang), jnp.sin(ang))


def _rope_tables(positions):
    s = positions.shape[0]
    inv = ROPE_THETA ** (-jnp.arange(0, HEAD_DIM, 2, dtype=F32) / HEAD_DIM)
    inv = jnp.tile(inv, LANES // (HEAD_DIM // 2)).reshape(1, LANES)
    pos = positions.astype(F32).reshape(s, 1)
    return pl.pallas_call(
        _rope_table_kernel,
        out_shape=(jax.ShapeDtypeStruct((s, LANES), F32), jax.ShapeDtypeStruct((s, LANES), F32)),
        name="rope_tables",
    )(pos, inv)


Q_SCALE = HEAD_DIM ** -0.5 * math.log2(math.e)
QKV_TM = 1024
QKV_TN = 512


def _qkv_proj_kernel(x_ref, g_ref, wq_ref, wk_ref, wv_ref, cos_ref, sin_ref, q_ref, k_ref, v_ref, xn_ref,
                     *scratch, d):
    j = pl.program_id(2)
    tm, dm = x_ref.shape[1], x_ref.shape[2]
    nt = tm // d

    def chain_rows(ref, r):
        return ref[pl.ds(r, nt, stride=d), :]

    @pl.when(j == 0)
    def _():
        xnorm = _rms_rows(x_ref[0], g_ref[...], NORM_EPS)
        if d == 1:
            xn_ref[...] = xnorm.astype(BF16)
        else:
            xs_ref = scratch[0]
            for c in range(dm // LANES):
                xs_ref[c] = xnorm[:, c * LANES:(c + 1) * LANES]
            for r in range(d):
                for c in range(dm // LANES):
                    xn_ref[r * nt:(r + 1) * nt, c * LANES:(c + 1) * LANES] = chain_rows(xs_ref.at[c], r).astype(BF16)

    if d == 1:
        ct, st = cos_ref[...], sin_ref[...]
    else:
        ct = jnp.concatenate([chain_rows(cos_ref, r) for r in range(d)], axis=0)
        st = jnp.concatenate([chain_rows(sin_ref, r) for r in range(d)], axis=0)
    lane = lax.broadcasted_iota(jnp.int32, (1, LANES), 1)
    first_half = (lane % HEAD_DIM) < (HEAD_DIM // 2)
    xn = xn_ref[...]
    parts = ((wq_ref, q_ref), (wk_ref, k_ref), (wv_ref, v_ref))
    matmul = lambda part: jnp.dot(xn, parts[part][0][...], preferred_element_type=F32)
    pending = matmul(0)
    for part, (w_ref, o_ref) in enumerate(parts):
        y = pending
        pending = matmul(part + 1) if part + 1 < len(parts) else None
        for c in range(w_ref.shape[1] // LANES):
            yc = y[:, c * LANES:(c + 1) * LANES]
            if part < 2:
                partner = jnp.where(first_half, pltpu.roll(yc, LANES - HEAD_DIM // 2, 1),
                                    pltpu.roll(yc, HEAD_DIM // 2, 1))
                yc = yc * ct + partner * st
            if part == 0:
                yc = yc * Q_SCALE
            for r in range(d):
                o_ref[0, c, r] = yc[r * nt:(r + 1) * nt, :].astype(BF16)


def _qkv_proj(h, gain, w, col0, cos, sin, dilation):
    b, s, dm = h.shape
    d = dilation
    n = s // d
    tm, tn = QKV_TM, QKV_TN
    nt = tm // d
    assert s % tm == 0 and dm % tn == 0 and col0 % tn == 0 and nt % 16 == 0
    c0 = col0 // tn
    per = dm // tn
    out = jax.ShapeDtypeStruct((b, dm // LANES, d, n, LANES), BF16)
    out_spec = pl.BlockSpec((1, tn // LANES, d, nt, LANES), lambda bi, i, j: (bi, j, 0, i, 0))
    scratch = [pltpu.VMEM((tm, dm), BF16)]
    if d > 1:
        scratch.append(pltpu.VMEM((dm // LANES, tm, LANES), F32))
    return pl.pallas_call(
        functools.partial(_qkv_proj_kernel, d=d),
        grid=(b, s // tm, per),
        in_specs=[
            pl.BlockSpec((1, tm, dm), lambda bi, i, j: (bi, i, 0)),
            pl.BlockSpec((1, dm), lambda bi, i, j: (0, 0)),
            pl.BlockSpec((dm, tn), lambda bi, i, j: (0, c0 + j)),
            pl.BlockSpec((dm, tn), lambda bi, i, j: (0, c0 + per + j)),
            pl.BlockSpec((dm, tn), lambda bi, i, j: (0, c0 + 2 * per + j)),
            pl.BlockSpec((tm, LANES), lambda bi, i, j: (i, 0)),
            pl.BlockSpec((tm, LANES), lambda bi, i, j: (i, 0)),
        ],
        out_specs=(out_spec, out_spec, out_spec),
        out_shape=(out, out, out),
        scratch_shapes=scratch,
        compiler_params=_params("parallel", "parallel", "arbitrary"),
        name=f"qkv_proj_d{d}",
    )(h, gain.reshape(1, dm), w, w, w, cos, sin)


def _out_proj_hm_kernel(a_ref, w_ref, h_ref, o_ref):
    a = jnp.concatenate([a_ref[0, p] for p in range(a_ref.shape[1])], axis=-1)
    o_ref[0] = h_ref[0] + jnp.dot(a, w_ref[...], preferred_element_type=F32)


def _out_proj_hm(a_hm, w, h):
    b, s, dm = h.shape
    kb = a_hm.shape[1]
    tm = 1024
    return pl.pallas_call(
        _out_proj_hm_kernel,
        grid=(b, s // tm),
        in_specs=[
            pl.BlockSpec((1, kb, tm, LANES), lambda bi, i: (bi, 0, i, 0)),
            pl.BlockSpec(w.shape, lambda bi, i: (0, 0)),
            pl.BlockSpec((1, tm, dm), lambda bi, i: (bi, i, 0)),
        ],
        out_specs=pl.BlockSpec((1, tm, dm), lambda bi, i: (bi, i, 0)),
        out_shape=jax.ShapeDtypeStruct(h.shape, F32),
        compiler_params=_params("parallel", "parallel"),
        name="out_proj_hm",
    )(a_hm, w, h)


def _out_proj_kernel(a_ref, w_ref, h_ref, o_ref):
    o_ref[0] = h_ref[0] + jnp.dot(a_ref[0], w_ref[...], preferred_element_type=F32)


def _out_proj(a, w, h):
    b, s, dm = h.shape
    k = a.shape[-1]
    tm = 1024
    return pl.pallas_call(
        _out_proj_kernel,
        grid=(b, s // tm),
        in_specs=[
            pl.BlockSpec((1, tm, k), lambda bi, i: (bi, i, 0)),
            pl.BlockSpec(w.shape, lambda bi, i: (0, 0)),
            pl.BlockSpec((1, tm, dm), lambda bi, i: (bi, i, 0)),
        ],
        out_specs=pl.BlockSpec((1, tm, dm), lambda bi, i: (bi, i, 0)),
        out_shape=jax.ShapeDtypeStruct(h.shape, F32),
        compiler_params=_params("parallel", "parallel"),
        name="out_proj",
    )(a, w, h)


POOL_PAD = 8
POOL_ROWS = 256


def _pool_kernel(h_ref, g_ref, w_ref, sc_ref, o_ref, hn_ref):
    s, dm = h_ref.shape[1], h_ref.shape[2]
    group = dm // len(POOL_WINDOWS)
    zeros = jnp.zeros((POOL_PAD, dm), F32)
    hn_ref[0:POOL_PAD, :] = zeros
    hn_ref[POOL_PAD + s:POOL_PAD + s + POOL_PAD, :] = zeros

    def norm_step(i, carry):
        r0 = pl.multiple_of(i * POOL_ROWS, POOL_ROWS)
        hn_ref[pl.ds(POOL_PAD + r0, POOL_ROWS), :] = _rms_rows(h_ref[0, pl.ds(r0, POOL_ROWS), :], g_ref[...], NORM_EPS)
        return carry

    lax.fori_loop(0, s // POOL_ROWS, norm_step, 0)

    def mix_step(i, carry):
        r0 = pl.multiple_of(i * POOL_ROWS, POOL_ROWS)
        t = r0 + lax.broadcasted_iota(jnp.int32, (POOL_ROWS, 1), 0)
        inner = slice(SUBLANES, SUBLANES + POOL_ROWS)
        for g, win in enumerate(POOL_WINDOWS):
            half = win // 2
            sl = slice(g * group, (g + 1) * group)
            xw = _halo_window(hn_ref, POOL_PAD + r0, POOL_ROWS, sl)
            tot = xw + _shift_rows(xw, -1)
            step = 1
            while 2 * step < win:
                tot = _shift_rows(tot, -step) + _shift_rows(tot, step)
                step *= 2
            cnt = (jnp.minimum(t + half, s) - jnp.maximum(t - half, 0)).astype(F32)
            pooled = (tot[inner] / cnt - xw[inner]).astype(BF16)
            y = jnp.dot(pooled, w_ref[g], preferred_element_type=F32) * sc_ref[:, sl]
            o_ref[0, pl.ds(r0, POOL_ROWS), sl] = h_ref[0, pl.ds(r0, POOL_ROWS), sl] + y
        return carry

    lax.fori_loop(0, s // POOL_ROWS, mix_step, 0)


def _pool_mixer(h, gain, w, scale):
    b, s, dm = h.shape
    return pl.pallas_call(
        _pool_kernel,
        grid=(b,),
        in_specs=[
            pl.BlockSpec((1, s, dm), lambda bi: (bi, 0, 0)),
            pl.BlockSpec((1, dm), lambda bi: (0, 0)),
            pl.BlockSpec(w.shape, lambda bi: (0, 0, 0)),
            pl.BlockSpec((1, dm), lambda bi: (0, 0)),
        ],
        out_specs=pl.BlockSpec((1, s, dm), lambda bi: (bi, 0, 0)),
        out_shape=jax.ShapeDtypeStruct(h.shape, F32),
        scratch_shapes=[pltpu.VMEM((s + 2 * POOL_PAD, dm), F32)],
        compiler_params=_params("parallel"),
        name="pool_mixer",
    )(h, gain.reshape(1, dm), w.astype(BF16), scale.reshape(1, dm))


DIFF_TQ = 256
DIFF_CHAINS = 4
DIFF_AHEAD = 2


def _diff_attn_kernel(q_ref, k_ref, v_ref, lam_ref, sub_ref, o_ref, *, lam_init):
    k = k_ref[0, 0, 0]
    v = v_ref[0, 0, 0]
    lp = lam_ref[...]
    lam = (jnp.exp(jnp.sum(lp[0:1] * lp[1:2], axis=-1, keepdims=True))
           - jnp.exp(jnp.sum(lp[2:3] * lp[3:4], axis=-1, keepdims=True)) + lam_init)
    lane = lax.broadcasted_iota(jnp.int32, (1, LANES), 1)
    first = lane < HEAD_DIM

    def scores(c):
        q = q_ref[0, 0, 0, c * DIFF_TQ:(c + 1) * DIFF_TQ, :]
        zero = jnp.zeros_like(q)
        dn = (((1,), (1,)), ((), ()))
        return (lax.dot_general(jnp.where(first, q, zero), k, dn, preferred_element_type=F32),
                lax.dot_general(jnp.where(first, zero, q), k, dn, preferred_element_type=F32))

    def softmax_parts(sc):
        p = jnp.exp2(sc - jnp.max(sc, axis=-1, keepdims=True))
        return p.astype(BF16), jnp.sum(p, axis=-1, keepdims=True)

    pending = [scores(c) for c in range(min(DIFF_AHEAD, DIFF_CHAINS))]
    for c in range(DIFF_CHAINS):
        if c + DIFF_AHEAD < DIFF_CHAINS:
            pending.append(scores(c + DIFF_AHEAD))
        sc = pending.pop(0)
        p0, l0 = softmax_parts(sc[0])
        p1, l1 = softmax_parts(sc[1])
        a = p0 * (1.0 / l0).astype(BF16) - p1 * (lam / l1).astype(BF16)
        o = jnp.dot(a, v, preferred_element_type=F32)
        o = _rms_rows(o, sub_ref[...], DIFF_SUBLN_EPS) * (1.0 - lam_init)
        o_ref[0, 0, c * DIFF_TQ:(c + 1) * DIFF_TQ, :] = o.astype(BF16)


def _diff_attention(q, k, v, lam_params, subln, lam_init):
    b, heads, _, s, _ = q.shape
    tq = DIFF_TQ * DIFF_CHAINS
    return pl.pallas_call(
        functools.partial(_diff_attn_kernel, lam_init=lam_init),
        grid=(b, heads, s // tq),
        in_specs=[
            pl.BlockSpec((1, 1, 1, tq, LANES), lambda bi, h, i: (bi, h, 0, i, 0)),
            pl.BlockSpec((1, 1, 1, s, LANES), lambda bi, h, i: (bi, h, 0, 0, 0)),
            pl.BlockSpec((1, 1, 1, s, LANES), lambda bi, h, i: (bi, h, 0, 0, 0)),
            pl.BlockSpec(lam_params.shape, lambda bi, h, i: (0, 0)),
            pl.BlockSpec((1, LANES), lambda bi, h, i: (0, 0)),
        ],
        out_specs=pl.BlockSpec((1, 1, tq, LANES), lambda bi, h, i: (bi, h, i, 0)),
        out_shape=jax.ShapeDtypeStruct((b, heads, s, LANES), BF16),
        compiler_params=_params("parallel", "parallel", "parallel"),
        name="diff_attention",
    )(q, k, v, lam_params, subln.reshape(1, LANES))


LRU_PAD = 8
LRU_ROWS = 256
LRU_PROJ_ROWS = 1024
LRU_CW = 256
LRU_SCAN_TILES = 8


def _log_sigmoid(x):
    return jnp.minimum(x, 0.0) - jnp.log1p(jnp.exp(-jnp.abs(x)))


def _gelu_tanh(x):
    return 0.5 * x * (1.0 + jnp.tanh(math.sqrt(2.0 / math.pi) * (x + 0.044715 * (x * x * x))))


def _lru_kernel(h_ref, g_ref, wg_ref, wu_ref, cw_ref, cb_ref, wbd_ref, bias_ref, lam_ref, o_ref,
                xn_ref, gate_ref, upad_ref, a_ref, b_ref, c_ref):
    s = h_ref.shape[1]
    cw = wg_ref.shape[1]
    n_slab = cw // LANES
    assert s == SUBLANES * LRU_ROWS

    @pl.when(pl.program_id(1) == 0)
    def _():
        def norm_step(i, carry):
            r0 = pl.multiple_of(i * LRU_ROWS, LRU_ROWS)
            xn_ref[pl.ds(r0, LRU_ROWS), :] = _rms_rows(h_ref[0, pl.ds(r0, LRU_ROWS), :], g_ref[...],
                                                       NORM_EPS).astype(BF16)
            return carry

        lax.fori_loop(0, s // LRU_ROWS, norm_step, 0)

    zeros = jnp.zeros((LRU_PAD, cw), F32)
    upad_ref[0:LRU_PAD, :] = zeros
    upad_ref[LRU_PAD + s:LRU_PAD + s + LRU_PAD, :] = zeros
    wgu = jnp.concatenate([wg_ref[...], wu_ref[...]], axis=-1)
    for blk in range(s // LRU_PROJ_ROWS):
        rows = slice(blk * LRU_PROJ_ROWS, (blk + 1) * LRU_PROJ_ROWS)
        gu = jnp.dot(xn_ref[rows, :], wgu, preferred_element_type=F32)
        gate_ref[rows, :] = gu[:, :cw]
        upad_ref[LRU_PAD + blk * LRU_PROJ_ROWS:LRU_PAD + (blk + 1) * LRU_PROJ_ROWS, :] = gu[:, cw:]

    row8 = lax.broadcasted_iota(jnp.int32, (SUBLANES, 1), 0)
    for dr in range(2):
        half_unit = (0.5 * LRU_C) * _log_sigmoid(lam_ref[dr:dr + 1, :])

        def gate_step(i, carry, dr=dr, half_unit=half_unit):
            r0 = pl.multiple_of(i * LRU_ROWS, LRU_ROWS)
            uw = _halo_window(upad_ref, LRU_PAD + r0, LRU_ROWS)
            xc = cb_ref[dr:dr + 1, :]
            for j in range(LRU_CONV):
                off = (j - (LRU_CONV - 1)) if dr == 0 else ((LRU_CONV - 1) - j)
                xc = xc + cw_ref[dr, j:j + 1, :] * _shift_rows(uw, off)[SUBLANES:SUBLANES + LRU_ROWS]
            pre = jnp.dot(xc.astype(BF16), wbd_ref[dr, 0], preferred_element_type=F32) + bias_ref[0, dr:dr + 1, :]
            t_r = jnp.tanh(pre[:, :cw])
            t_i = jnp.tanh(pre[:, cw:])
            log_a = t_r * half_unit + half_unit
            a = jnp.exp(log_a)
            th = jnp.tanh(log_a)
            bv = jnp.sqrt(-0.5 * th / (1.0 - th)) * ((t_i + 1.0) * xc)
            for sl in range(n_slab):
                lanes = slice(sl * LANES, (sl + 1) * LANES)
                a_ref[dr, sl, pl.ds(i, LRU_ROWS, stride=SUBLANES), :] = a[:, lanes]
                b_ref[dr, sl, pl.ds(i, LRU_ROWS, stride=SUBLANES), :] = bv[:, lanes]
            return carry

        lax.fori_loop(0, SUBLANES, gate_step, 0)

        n_blocks = LRU_ROWS // LRU_SCAN_TILES

        def scan_block(tb, carry, dr=dr):
            blk = tb if dr == 0 else n_blocks - 1 - tb
            base = pl.multiple_of(blk * (LRU_SCAN_TILES * SUBLANES), LRU_SCAN_TILES * SUBLANES)
            hs, ps = list(carry[0]), list(carry[1])
            for u in range(LRU_SCAN_TILES):
                tile = u if dr == 0 else LRU_SCAN_TILES - 1 - u
                rows = pl.ds(base + tile * SUBLANES, SUBLANES)
                for sl in range(n_slab):
                    a_t = a_ref[dr, sl, rows, :]
                    hs[sl] = a_t * hs[sl] + b_ref[dr, sl, rows, :]
                    ps[sl] = a_t * ps[sl]
                    b_ref[dr, sl, rows, :] = hs[sl]
                    a_ref[dr, sl, rows, :] = ps[sl]
            return tuple(hs), tuple(ps)

        init = (tuple(jnp.zeros((SUBLANES, LANES), F32) for _ in range(n_slab)),
                tuple(jnp.ones((SUBLANES, LANES), F32) for _ in range(n_slab)))
        h_end, p_end = lax.fori_loop(0, n_blocks, scan_block, init)

        for sl in range(n_slab):
            c = jnp.zeros((SUBLANES, LANES), F32)
            for _ in range(SUBLANES - 1):
                nxt = h_end[sl] + p_end[sl] * c
                if dr == 0:
                    c = jnp.where(row8 == 0, 0.0, _shift_rows(nxt, -1))
                else:
                    c = jnp.where(row8 == SUBLANES - 1, 0.0, _shift_rows(nxt, 1))
            c_ref[dr, sl] = c

    def fix_step(t, carry):
        r0 = pl.multiple_of(t * LRU_ROWS, LRU_ROWS)
        for sl in range(n_slab):
            tot = None
            for dr in range(2):
                local = b_ref[dr, sl, pl.ds(r0, LRU_ROWS), :].reshape(-1, SUBLANES, LANES)
                decay = a_ref[dr, sl, pl.ds(r0, LRU_ROWS), :].reshape(-1, SUBLANES, LANES)
                fixed = local + decay * c_ref[dr, sl]
                tot = fixed if tot is None else tot + fixed
            b_ref[0, sl, pl.ds(r0, LRU_ROWS), :] = tot.reshape(LRU_ROWS, LANES)
        return carry

    lax.fori_loop(0, SUBLANES, fix_step, 0)

    for i in range(SUBLANES):
        rows = slice(i * LRU_ROWS, (i + 1) * LRU_ROWS)
        hsum = jnp.concatenate([b_ref[0, sl, pl.ds(i, LRU_ROWS, stride=SUBLANES), :] for sl in range(n_slab)],
                               axis=-1)
        o_ref[0, rows, :] = (_gelu_tanh(gate_ref[rows, :]) * hsum).astype(BF16)


def _lru_core(h, gain, w_in, conv_w, conv_b, w_a, b_a, w_x, b_x, lam):
    b, s, dm = h.shape
    c = w_in.shape[1] // 2
    cw = LRU_CW
    ncb = c // cw
    per = cw // HEAD_DIM
    eye = jnp.eye(per, dtype=F32)

    def block_diag(w):
        w5 = w.reshape(2, ncb, per, HEAD_DIM, HEAD_DIM)
        return jnp.einsum('dcipq,ik->dcipkq', w5, eye).reshape(2, ncb, cw, cw)

    wbd = (0.5 * jnp.concatenate([block_diag(w_a), block_diag(w_x)], axis=-1)).astype(BF16)
    bias = 0.5 * jnp.concatenate([b_a.reshape(2, ncb, cw), b_x.reshape(2, ncb, cw)], axis=-1)
    bias = bias.transpose(1, 0, 2)
    n_slab = cw // LANES
    return pl.pallas_call(
        _lru_kernel,
        grid=(b, ncb),
        in_specs=[
            pl.BlockSpec((1, s, dm), lambda bi, cb: (bi, 0, 0), pipeline_mode=pl.Buffered(1)),
            pl.BlockSpec((1, dm), lambda bi, cb: (0, 0)),
            pl.BlockSpec((dm, cw), lambda bi, cb: (0, cb)),
            pl.BlockSpec((dm, cw), lambda bi, cb: (0, ncb + cb)),
            pl.BlockSpec((2, LRU_CONV, cw), lambda bi, cb: (0, 0, cb)),
            pl.BlockSpec((2, cw), lambda bi, cb: (0, cb)),
            pl.BlockSpec((2, 1, cw, 2 * cw), lambda bi, cb: (0, cb, 0, 0)),
            pl.BlockSpec((1, 2, 2 * cw), lambda bi, cb: (cb, 0, 0)),
            pl.BlockSpec((2, cw), lambda bi, cb: (0, cb)),
        ],
        out_specs=pl.BlockSpec((1, s, cw), lambda bi, cb: (bi, 0, cb)),
        out_shape=jax.ShapeDtypeStruct((b, s, c), BF16),
        scratch_shapes=[
            pltpu.VMEM((s, dm), BF16),
            pltpu.VMEM((s, cw), F32),
            pltpu.VMEM((s + 2 * LRU_PAD, cw), F32),
            pltpu.VMEM((2, n_slab, s, LANES), F32),
            pltpu.VMEM((2, n_slab, s, LANES), F32),
            pltpu.VMEM((2, n_slab, SUBLANES, LANES), F32),
        ],
        compiler_params=_params("parallel", "arbitrary"),
        name="lru_core",
    )(h, gain.reshape(1, dm), w_in, w_in, conv_w, conv_b, wbd, bias, lam)


DIL_TQ = 128
DIL_UNROLL = 4


def _band_bias(tq, w, rel0, half):
    qi = lax.broadcasted_iota(jnp.int32, (tq, 1), 0)
    ki = lax.broadcasted_iota(jnp.int32, (1, w), 1)
    return jnp.where(jnp.abs(ki + rel0 - qi) <= half, 0.0, NEG_INF).astype(F32)


def _band_scores(q, k):
    lane = lax.broadcasted_iota(jnp.int32, (1, LANES), 1)
    first = lane < HEAD_DIM
    zero = jnp.zeros_like(q)
    qq = jnp.concatenate([jnp.where(first, q, zero), jnp.where(first, zero, q)], axis=0)
    return lax.dot_general(qq, k, (((1,), (1,)), ((), ())), preferred_element_type=F32)


def _band_softmax(sc, bias):
    sc = sc + jnp.concatenate([bias, bias], axis=0)
    m = jnp.max(sc, axis=-1, keepdims=True)
    p = jnp.exp2(sc - m).astype(BF16)
    return p, jnp.broadcast_to(m, (sc.shape[0], LANES))


def _band_pv(p, m_b, v):
    tq = p.shape[0] // 2
    lane = lax.broadcasted_iota(jnp.int32, (1, LANES), 1)
    first = lane < HEAD_DIM
    v1 = jnp.concatenate([v, jnp.ones_like(v)], axis=-1)
    pv = jnp.dot(p, v1, preferred_element_type=F32)
    pick = lambda t: jnp.where(first, t[:tq], t[tq:])
    return pick(m_b), pick(pv[:, LANES:]), pick(pv[:, :LANES])


def _dilated_attn_kernel(q0_ref, k0_ref, v0_ref, q1_ref, k1_ref, v1_ref, q2_ref, k2_ref, v2_ref,
                         o_ref, m_ref, l_ref, acc_ref, bias_ref, p_ref, mb_ref, *, halves, dils):
    s = o_ref.shape[2]
    tq = DIL_TQ
    win = 2 * tq

    def merge(rows, m_new, l_new, acc_new):
        m_old, l_old, acc_old = m_ref[rows, :], l_ref[rows, :], acc_ref[rows, :]
        m_tot = jnp.maximum(m_old, m_new)
        alpha = jnp.exp2(m_old - m_tot)
        beta = jnp.exp2(m_new - m_tot)
        m_ref[rows, :] = m_tot
        l_ref[rows, :] = alpha * l_old + beta * l_new
        acc_ref[rows, :] = alpha * acc_old + beta * acc_new

    groups = ((q0_ref, k0_ref, v0_ref), (q1_ref, k1_ref, v1_ref), (q2_ref, k2_ref, v2_ref))
    order = sorted(range(len(groups)), key=lambda gi: -dils[gi])
    for g in order:
        q_ref, k_ref, v_ref = groups[g]
        half, d = halves[g], dils[g]
        n = s // d
        tiles = n // tq
        w = min(win, n)
        for case, rel0 in enumerate((0, -(tq // 2), tq - w) if tiles > 1 else (0,)):
            bias_ref[case, :, 0:w] = _band_bias(tq, w, rel0, half)

        def tile_coords(c, n=n, tiles=tiles, w=w):
            r = c // tiles
            i = c % tiles
            q0 = pl.multiple_of(i * tq, tq)
            ks = pl.multiple_of(jnp.clip(q0 - tq // 2, 0, n - w), tq // 2)
            case = jnp.where(i == 0, 0, jnp.where(i == tiles - 1, 2, 1)) if tiles > 1 else 0
            return r, q0, ks, case

        def score_stage(cb, slot, q_ref=q_ref, k_ref=k_ref, w=w):
            for u in range(DIL_UNROLL):
                r, q0, ks, case = tile_coords(cb * DIL_UNROLL + u)
                sc = _band_scores(q_ref[0, 0, r, pl.ds(q0, tq), :], k_ref[0, 0, r, pl.ds(ks, w), :])
                p, m_b = _band_softmax(sc, bias_ref[case, :, 0:w])
                p_ref[slot, u, :, 0:w] = p
                mb_ref[slot, u] = m_b

        def value_stage(cb, slot, g=g, v_ref=v_ref, d=d, w=w):
            for u in range(DIL_UNROLL):
                r, q0, ks, _ = tile_coords(cb * DIL_UNROLL + u)
                m_new, l_new, acc_new = _band_pv(p_ref[slot, u, :, 0:w], mb_ref[slot, u],
                                                 v_ref[0, 0, r, pl.ds(ks, w), :])
                rows = pl.ds(q0, tq) if d == 1 else pl.ds(q0 * d + r, tq, stride=d)
                if g == order[0]:
                    m_ref[rows, :] = m_new
                    l_ref[rows, :] = l_new
                    acc_ref[rows, :] = acc_new
                else:
                    merge(rows, m_new, l_new, acc_new)

        assert (d * tiles) % DIL_UNROLL == 0
        n_batches = d * tiles // DIL_UNROLL
        score_stage(0, 0)

        def pipe_step(cb, carry, score_stage=score_stage, value_stage=value_stage):
            value_stage(cb - 1, (cb - 1) % 2)
            score_stage(cb, cb % 2)
            return carry

        lax.fori_loop(1, n_batches, pipe_step, 0)
        value_stage(n_batches - 1, (n_batches - 1) % 2)

    def out_step(i, carry):
        r0 = pl.multiple_of(i * 256, 256)
        o_ref[0, 0, pl.ds(r0, 256), :] = (acc_ref[pl.ds(r0, 256), :] / l_ref[pl.ds(r0, 256), :]).astype(BF16)
        return carry

    lax.fori_loop(0, s // 256, out_step, 0)


def _dilated_attention(qkv_groups):
    b, pairs, d0, s, _ = qkv_groups[0][0].shape
    assert d0 == 1
    halves = tuple(w // (2 * d) for (w, d) in DIL_GROUPS)
    dils = tuple(d for (_, d) in DIL_GROUPS)
    in_specs, args = [], []
    for q, k, v in qkv_groups:
        d, n = q.shape[2], q.shape[3]
        for arr in (q, k, v):
            in_specs.append(pl.BlockSpec((1, 1, d, n, LANES), lambda bi, p: (bi, p, 0, 0, 0)))
            args.append(arr)
    return pl.pallas_call(
        functools.partial(_dilated_attn_kernel, halves=halves, dils=dils),
        grid=(b, pairs),
        in_specs=in_specs,
        out_specs=pl.BlockSpec((1, 1, s, LANES), lambda bi, p: (bi, p, 0, 0)),
        out_shape=jax.ShapeDtypeStruct((b, pairs, s, LANES), BF16),
        scratch_shapes=[pltpu.VMEM((s, LANES), F32), pltpu.VMEM((s, LANES), F32), pltpu.VMEM((s, LANES), F32),
                        pltpu.VMEM((3, DIL_TQ, 2 * DIL_TQ), F32),
                        pltpu.VMEM((2, DIL_UNROLL, 2 * DIL_TQ, 2 * DIL_TQ), BF16),
                        pltpu.VMEM((2, DIL_UNROLL, 2 * DIL_TQ, LANES), F32)],
        compiler_params=_params("parallel", "parallel"),
        name="dilated_attention",
    )(*args)


FFN_HALO = 16
FFN_UP_ROWS = 1024
FFN_DOWN_ROWS = 512
FFN_CHUNK = 256
FFN_STEP_CHUNKS = 4


def _ffn_kernel(h_ref, g_ref, wg_ref, *rest, final_norm, n_chunks):
    wu_refs = rest[:FFN_STEP_CHUNKS]
    cw_ref, cb_ref, wd_ref, hr_ref, fg_ref, o_ref, xn_ref, hid_ref = rest[FFN_STEP_CHUNKS:]
    j = pl.program_id(1)
    s = h_ref.shape[1]
    n_full = n_chunks // FFN_STEP_CHUNKS
    n_up = -(-n_chunks // FFN_STEP_CHUNKS)

    @pl.when(j == 0)
    def _():
        zeros = jnp.zeros((FFN_HALO, xn_ref.shape[1]), BF16)
        xn_ref[0:FFN_HALO, :] = zeros
        xn_ref[FFN_HALO + s:FFN_HALO + s + FFN_HALO, :] = zeros

        def norm_step(i, carry):
            r0 = pl.multiple_of(i * FFN_DOWN_ROWS, FFN_DOWN_ROWS)
            x = h_ref[0, pl.ds(r0, FFN_DOWN_ROWS), :]
            xn_ref[pl.ds(FFN_HALO + r0, FFN_DOWN_ROWS), :] = _rms_rows(x, g_ref[...], NORM_EPS).astype(BF16)
            return carry

        lax.fori_loop(0, s // FFN_DOWN_ROWS, norm_step, 0)

    def up_chunks(n_here):
        inner = slice(FFN_HALO, FFN_HALO + FFN_UP_ROWS)
        jobs = []
        for k in range(n_here):
            c = j * FFN_STEP_CHUNKS + k
            cols = slice(k * FFN_CHUNK, (k + 1) * FFN_CHUNK)
            wgu = jnp.concatenate([wg_ref[:, cols], wu_refs[k][...]], axis=-1)
            jobs += [(c, cols, wgu, rc) for rc in range(s // FFN_UP_ROWS)]

        def matmul(job):
            _, _, wgu, rc = job
            xs = xn_ref[rc * FFN_UP_ROWS:rc * FFN_UP_ROWS + FFN_UP_ROWS + 2 * FFN_HALO, :]
            return jnp.dot(xs, wgu, preferred_element_type=F32)

        def epilogue(job, gu):
            c, cols, _, rc = job
            g = gu[:, :FFN_CHUNK]
            gc = cb_ref[:, cols]
            for t in range(FFN_CONV):
                gc = gc + cw_ref[t:t + 1, cols] * _shift_rows(g, t - FFN_CONV // 2)[inner]
            act = 0.5 * gc * (1.0 + lax.erf(gc * (1.0 / math.sqrt(2.0))))
            hid_ref[c, rc * FFN_UP_ROWS:(rc + 1) * FFN_UP_ROWS, :] = (act * gu[inner, FFN_CHUNK:]).astype(BF16)

        pending = matmul(jobs[0])
        for n, job in enumerate(jobs):
            nxt = matmul(jobs[n + 1]) if n + 1 < len(jobs) else None
            epilogue(job, pending)
            pending = nxt

    @pl.when(j < n_full)
    def _():
        up_chunks(FFN_STEP_CHUNKS)

    if n_chunks % FFN_STEP_CHUNKS:
        @pl.when(j == n_full)
        def _():
            up_chunks(n_chunks % FFN_STEP_CHUNKS)

    @pl.when(j >= n_up)
    def _():
        r0 = pl.multiple_of((j - n_up) * FFN_DOWN_ROWS, FFN_DOWN_ROWS)
        hid = jnp.concatenate([hid_ref[c, pl.ds(r0, FFN_DOWN_ROWS), :] for c in range(n_chunks)], axis=-1)
        out = hr_ref[0] + jnp.dot(hid, wd_ref[...], preferred_element_type=F32)
        if final_norm:
            out = _rms_rows(out, fg_ref[...], NORM_EPS)
        o_ref[0] = out


def _conv_ffn(h, gain, w_up, conv_w, conv_b, w_down, final_gain):
    b, s, dm = h.shape
    dff = w_down.shape[0]
    n_chunks = dff // FFN_CHUNK
    per = FFN_STEP_CHUNKS
    n_up = -(-n_chunks // per)
    n_down = s // FFN_DOWN_ROWS
    assert dff % FFN_CHUNK == 0 and s % FFN_UP_ROWS == 0 and s % FFN_DOWN_ROWS == 0
    final_norm = final_gain is not None
    fg = (final_gain if final_norm else gain).reshape(1, dm)
    pad = per * n_up * FFN_CHUNK - dff
    conv_w = jnp.pad(conv_w, ((0, 0), (0, pad)))
    conv_b = jnp.pad(conv_b.reshape(1, dff), ((0, 0), (0, pad)))
    up_idx = lambda j: jnp.minimum(j, n_up - 1)
    down_idx = lambda j: jnp.maximum(j - n_up, 0)
    value_specs = [pl.BlockSpec((dm, FFN_CHUNK), lambda bi, j, k=k: (0, jnp.minimum(n_chunks + per * up_idx(j) + k,
                                                                                   2 * n_chunks - 1)))
                   for k in range(per)]
    return pl.pallas_call(
        functools.partial(_ffn_kernel, final_norm=final_norm, n_chunks=n_chunks),
        grid=(b, n_up + n_down),
        in_specs=[
            pl.BlockSpec((1, s, dm), lambda bi, j: (bi, 0, 0)),
            pl.BlockSpec((1, dm), lambda bi, j: (0, 0)),
            pl.BlockSpec((dm, per * FFN_CHUNK), lambda bi, j: (0, up_idx(j))),
            *value_specs,
            pl.BlockSpec((FFN_CONV, per * FFN_CHUNK), lambda bi, j: (0, up_idx(j))),
            pl.BlockSpec((1, per * FFN_CHUNK), lambda bi, j: (0, up_idx(j))),
            pl.BlockSpec((dff, dm), lambda bi, j: (0, 0), pipeline_mode=pl.Buffered(1)),
            pl.BlockSpec((1, FFN_DOWN_ROWS, dm), lambda bi, j: (bi, down_idx(j), 0)),
            pl.BlockSpec((1, dm), lambda bi, j: (0, 0)),
        ],
        out_specs=pl.BlockSpec((1, FFN_DOWN_ROWS, dm), lambda bi, j: (bi, down_idx(j), 0)),
        out_shape=jax.ShapeDtypeStruct(h.shape, F32),
        scratch_shapes=[pltpu.VMEM((s + 2 * FFN_HALO, dm), BF16), pltpu.VMEM((n_chunks, s, FFN_CHUNK), BF16)],
        compiler_params=_params("parallel", "arbitrary"),
        name="conv_ffn",
    )(h, gain.reshape(1, dm), w_up, *([w_up] * per), conv_w, conv_b, w_down, h, fg)


def kernel(x, positions, mix_norm, pool_w, pool_scale, diff_w_qkv, diff_lam_q1, diff_lam_k1, diff_lam_q2,
           diff_lam_k2, diff_subln, diff_w_o, lru_w_in, lru_conv_w, lru_conv_b, lru_w_a, lru_b_a, lru_w_x,
           lru_b_x, lru_lambda, lru_w_out, dil_w_qkv, dil_w_o, ffn_norm, ffn_w_up, ffn_conv_w, ffn_conv_b,
           ffn_w_down, final_norm):
    b, s, dm = x.shape
    depth = mix_norm.shape[0]
    cos, sin = _rope_tables(positions)
    h = x
    for i in range(depth):
        m, j = i % N_MIXERS, i // N_MIXERS
        if m == 0:
            h = _pool_mixer(h, mix_norm[i], pool_w[j], pool_scale[j])
        elif m == 1:
            q, k, v = _qkv_proj(h, mix_norm[i], diff_w_qkv[j].astype(BF16), 0, cos, sin, 1)
            lam_params = jnp.zeros((SUBLANES, LANES), F32)
            lam_params = lam_params.at[0:4, 0:HEAD_DIM].set(
                jnp.stack([diff_lam_q1[j], diff_lam_k1[j], diff_lam_q2[j], diff_lam_k2[j]]).astype(F32))
            lam_init = 0.8 - 0.6 * math.exp(-0.3 * i)
            o = _diff_attention(q, k, v, lam_params, diff_subln[j], lam_init)
            h = _out_proj_hm(o, diff_w_o[j].astype(BF16), h)
        elif m == 2:
            y = _lru_core(h, mix_norm[i], lru_w_in[j].astype(BF16), lru_conv_w[j], lru_conv_b[j], lru_w_a[j],
                          lru_b_a[j], lru_w_x[j], lru_b_x[j], lru_lambda[j])
            h = _out_proj(y, lru_w_out[j].astype(BF16), h)
        else:
            w = dil_w_qkv[j].astype(BF16)
            groups = [_qkv_proj(h, mix_norm[i], w, g * 3 * dm, cos, sin, d)
                      for g, (_, d) in enumerate(DIL_GROUPS)]
            o = _dilated_attention(groups)
            h = _out_proj_hm(o, dil_w_o[j].astype(BF16), h)
        h = _conv_ffn(h, ffn_norm[i], ffn_w_up[i].astype(BF16), ffn_conv_w[i], ffn_conv_b[i],
                      ffn_w_down[i].astype(BF16), final_norm if i == depth - 1 else None)
    return h
```

```python
import functools
import math

import jax
import jax.numpy as jnp
from jax import lax
from jax.experimental import pallas as pl
from jax.experimental.pallas import tpu as pltpu

F32 = jnp.float32
BF16 = jnp.bfloat16

HEAD_DIM = 64
NORM_EPS = 1e-6
ROPE_THETA = 10000.0
NEG_INF = -1e30
POOL_WINDOWS = (2, 4, 8, 16)
DIFF_SUBLN_EPS = 1e-5
LRU_C = 8.0
LRU_CONV = 4
DIL_GROUPS = ((128, 1), (512, 4), (2048, 16))
FFN_CONV = 3
N_MIXERS = 4

LANES = 128
SUBLANES = 8
VMEM_LIMIT_BYTES = 56 * 1024 * 1024


def _params(*semantics):
    return pltpu.CompilerParams(dimension_semantics=semantics, vmem_limit_bytes=VMEM_LIMIT_BYTES)


def _rms_rows(x, gain, eps):
    ms = jnp.mean(x * x, axis=-1, keepdims=True)
    return x * lax.rsqrt(ms + eps) * gain


def _halo_window(ref, r0, rows, cols=slice(None)):
    return ref[pl.ds(r0 - SUBLANES, rows + 2 * SUBLANES), cols]


def _shift_rows(win, off):
    if off == 0:
        return win
    return pltpu.roll(win, (-off) % win.shape[0], 0)


def _rope_table_kernel(pos_ref, inv_ref, cos_ref, sin_ref):
    ang = pos_ref[...] * inv_ref[...]
    lane = lax.broadcasted_iota(jnp.int32, ang.shape, 1)
    first_half = (lane % HEAD_DIM) < (HEAD_DIM // 2)
    cos_ref[...] = jnp.cos(ang)
    sin_ref[...] = jnp.where(first_half, -jnp.sin(ang), jnp.sin(ang))


def _rope_tables(positions):
    s = positions.shape[0]
    inv = ROPE_THETA ** (-jnp.arange(0, HEAD_DIM, 2, dtype=F32) / HEAD_DIM)
    inv = jnp.tile(inv, LANES // (HEAD_DIM // 2)).reshape(1, LANES)
    pos = positions.astype(F32).reshape(s, 1)
    return pl.pallas_call(
        _rope_table_kernel,
        out_shape=(jax.ShapeDtypeStruct((s, LANES), F32), jax.ShapeDtypeStruct((s, LANES), F32)),
        name="rope_tables",
    )(pos, inv)


Q_SCALE = HEAD_DIM ** -0.5 * math.log2(math.e)
QKV_TM = 1024
QKV_TN = 512


def _qkv_proj_kernel(x_ref, g_ref, wq_ref, wk_ref, wv_ref, cos_ref, sin_ref, q_ref, k_ref, v_ref, xn_ref,
                     *scratch, d):
    j = pl.program_id(2)
    tm, dm = x_ref.shape[1], x_ref.shape[2]
    nt = tm // d

    def chain_rows(ref, r):
        return ref[pl.ds(r, nt, stride=d), :]

    @pl.when(j == 0)
    def _():
        xnorm = _rms_rows(x_ref[0], g_ref[...], NORM_EPS)
        if d == 1:
            xn_ref[...] = xnorm.astype(BF16)
        else:
            xs_ref = scratch[0]
            for c in range(dm // LANES):
                xs_ref[c] = xnorm[:, c * LANES:(c + 1) * LANES]
            for r in range(d):
                for c in range(dm // LANES):
                    xn_ref[r * nt:(r + 1) * nt, c * LANES:(c + 1) * LANES] = chain_rows(xs_ref.at[c], r).astype(BF16)

    if d == 1:
        ct, st = cos_ref[...], sin_ref[...]
    else:
        ct = jnp.concatenate([chain_rows(cos_ref, r) for r in range(d)], axis=0)
        st = jnp.concatenate([chain_rows(sin_ref, r) for r in range(d)], axis=0)
    lane = lax.broadcasted_iota(jnp.int32, (1, LANES), 1)
    first_half = (lane % HEAD_DIM) < (HEAD_DIM // 2)
    xn = xn_ref[...]
    parts = ((wq_ref, q_ref), (wk_ref, k_ref), (wv_ref, v_ref))
    matmul = lambda part: jnp.dot(xn, parts[part][0][...], preferred_element_type=F32)
    pending = matmul(0)
    for part, (w_ref, o_ref) in enumerate(parts):
        y = pending
        pending = matmul(part + 1) if part + 1 < len(parts) else None
        for c in range(w_ref.shape[1] // LANES):
            yc = y[:, c * LANES:(c + 1) * LANES]
            if part < 2:
                partner = jnp.where(first_half, pltpu.roll(yc, LANES - HEAD_DIM // 2, 1),
                                    pltpu.roll(yc, HEAD_DIM // 2, 1))
                yc = yc * ct + partner * st
            if part == 0:
                yc = yc * Q_SCALE
            for r in range(d):
                o_ref[0, c, r] = yc[r * nt:(r + 1) * nt, :].astype(BF16)


def _qkv_proj(h, gain, w, col0, cos, sin, dilation):
    b, s, dm = h.shape
    d = dilation
    n = s // d
    tm, tn = QKV_TM, QKV_TN
    nt = tm // d
    assert s % tm == 0 and dm % tn == 0 and col0 % tn == 0 and nt % 16 == 0
    c0 = col0 // tn
    per = dm // tn
    out = jax.ShapeDtypeStruct((b, dm // LANES, d, n, LANES), BF16)
    out_spec = pl.BlockSpec((1, tn // LANES, d, nt, LANES), lambda bi, i, j: (bi, j, 0, i, 0))
    scratch = [pltpu.VMEM((tm, dm), BF16)]
    if d > 1:
        scratch.append(pltpu.VMEM((dm // LANES, tm, LANES), F32))
    return pl.pallas_call(
        functools.partial(_qkv_proj_kernel, d=d),
        grid=(b, s // tm, per),
        in_specs=[
            pl.BlockSpec((1, tm, dm), lambda bi, i, j: (bi, i, 0)),
            pl.BlockSpec((1, dm), lambda bi, i, j: (0, 0)),
            pl.BlockSpec((dm, tn), lambda bi, i, j: (0, c0 + j)),
            pl.BlockSpec((dm, tn), lambda bi, i, j: (0, c0 + per + j)),
            pl.BlockSpec((dm, tn), lambda bi, i, j: (0, c0 + 2 * per + j)),
            pl.BlockSpec((tm, LANES), lambda bi, i, j: (i, 0)),
            pl.BlockSpec((tm, LANES), lambda bi, i, j: (i, 0)),
        ],
        out_specs=(out_spec, out_spec, out_spec),
        out_shape=(out, out, out),
        scratch_shapes=scratch,
        compiler_params=_params("parallel", "parallel", "arbitrary"),
        name=f"qkv_proj_d{d}",
    )(h, gain.reshape(1, dm), w, w, w, cos, sin)


def _out_proj_hm_kernel(a_ref, w_ref, h_ref, o_ref):
    a = jnp.concatenate([a_ref[0, p] for p in range(a_ref.shape[1])], axis=-1)
    o_ref[0] = h_ref[0] + jnp.dot(a, w_ref[...], preferred_element_type=F32)


def _out_proj_hm(a_hm, w, h):
    b, s, dm = h.shape
    kb = a_hm.shape[1]
    tm = 1024
    return pl.pallas_call(
        _out_proj_hm_kernel,
        grid=(b, s // tm),
        in_specs=[
            pl.BlockSpec((1, kb, tm, LANES), lambda bi, i: (bi, 0, i, 0)),
            pl.BlockSpec(w.shape, lambda bi, i: (0, 0)),
            pl.BlockSpec((1, tm, dm), lambda bi, i: (bi, i, 0)),
        ],
        out_specs=pl.BlockSpec((1, tm, dm), lambda bi, i: (bi, i, 0)),
        out_shape=jax.ShapeDtypeStruct(h.shape, F32),
        compiler_params=_params("parallel", "parallel"),
        name="out_proj_hm",
    )(a_hm, w, h)


def _out_proj_kernel(a_ref, w_ref, h_ref, o_ref):
    o_ref[0] = h_ref[0] + jnp.dot(a_ref[0], w_ref[...], preferred_element_type=F32)


def _out_proj(a, w, h):
    b, s, dm = h.shape
    k = a.shape[-1]
    tm = 1024
    return pl.pallas_call(
        _out_proj_kernel,
        grid=(b, s // tm),
        in_specs=[
            pl.BlockSpec((1, tm, k), lambda bi, i: (bi, i, 0)),
            pl.BlockSpec(w.shape, lambda bi, i: (0, 0)),
            pl.BlockSpec((1, tm, dm), lambda bi, i: (bi, i, 0)),
        ],
        out_specs=pl.BlockSpec((1, tm, dm), lambda bi, i: (bi, i, 0)),
        out_shape=jax.ShapeDtypeStruct(h.shape, F32),
        compiler_params=_params("parallel", "parallel"),
        name="out_proj",
    )(a, w, h)


POOL_PAD = 8
POOL_ROWS = 256


def _pool_kernel(h_ref, g_ref, w_ref, sc_ref, o_ref, hn_ref):
    s, dm = h_ref.shape[1], h_ref.shape[2]
    group = dm // len(POOL_WINDOWS)
    zeros = jnp.zeros((POOL_PAD, dm), F32)
    hn_ref[0:POOL_PAD, :] = zeros
    hn_ref[POOL_PAD + s:POOL_PAD + s + POOL_PAD, :] = zeros

    def norm_step(i, carry):
        r0 = pl.multiple_of(i * POOL_ROWS, POOL_ROWS)
        hn_ref[pl.ds(POOL_PAD + r0, POOL_ROWS), :] = _rms_rows(h_ref[0, pl.ds(r0, POOL_ROWS), :], g_ref[...], NORM_EPS)
        return carry

    lax.fori_loop(0, s // POOL_ROWS, norm_step, 0)

    def mix_step(i, carry):
        r0 = pl.multiple_of(i * POOL_ROWS, POOL_ROWS)
        t = r0 + lax.broadcasted_iota(jnp.int32, (POOL_ROWS, 1), 0)
        inner = slice(SUBLANES, SUBLANES + POOL_ROWS)
        for g, win in enumerate(POOL_WINDOWS):
            half = win // 2
            sl = slice(g * group, (g + 1) * group)
            xw = _halo_window(hn_ref, POOL_PAD + r0, POOL_ROWS, sl)
            tot = xw + _shift_rows(xw, -1)
            step = 1
            while 2 * step < win:
                tot = _shift_rows(tot, -step) + _shift_rows(tot, step)
                step *= 2
            cnt = (jnp.minimum(t + half, s) - jnp.maximum(t - half, 0)).astype(F32)
            pooled = (tot[inner] / cnt - xw[inner]).astype(BF16)
            y = jnp.dot(pooled, w_ref[g], preferred_element_type=F32) * sc_ref[:, sl]
            o_ref[0, pl.ds(r0, POOL_ROWS), sl] = h_ref[0, pl.ds(r0, POOL_ROWS), sl] + y
        return carry

    lax.fori_loop(0, s // POOL_ROWS, mix_step, 0)


def _pool_mixer(h, gain, w, scale):
    b, s, dm = h.shape
    return pl.pallas_call(
        _pool_kernel,
        grid=(b,),
        in_specs=[
            pl.BlockSpec((1, s, dm), lambda bi: (bi, 0, 0)),
            pl.BlockSpec((1, dm), lambda bi: (0, 0)),
            pl.BlockSpec(w.shape, lambda bi: (0, 0, 0)),
            pl.BlockSpec((1, dm), lambda bi: (0, 0)),
        ],
        out_specs=pl.BlockSpec((1, s, dm), lambda bi: (bi, 0, 0)),
        out_shape=jax.ShapeDtypeStruct(h.shape, F32),
        scratch_shapes=[pltpu.VMEM((s + 2 * POOL_PAD, dm), F32)],
        compiler_params=_params("parallel"),
        name="pool_mixer",
    )(h, gain.reshape(1, dm), w.astype(BF16), scale.reshape(1, dm))


DIFF_TQ = 256
DIFF_CHAINS = 4
DIFF_AHEAD = 2


def _diff_attn_kernel(q_ref, k_ref, v_ref, lam_ref, sub_ref, o_ref, *, lam_init):
    k = k_ref[0, 0, 0]
    v = v_ref[0, 0, 0]
    lp = lam_ref[...]
    lam = (jnp.exp(jnp.sum(lp[0:1] * lp[1:2], axis=-1, keepdims=True))
           - jnp.exp(jnp.sum(lp[2:3] * lp[3:4], axis=-1, keepdims=True)) + lam_init)
    lane = lax.broadcasted_iota(jnp.int32, (1, LANES), 1)
    first = lane < HEAD_DIM

    def scores(c):
        q = q_ref[0, 0, 0, c * DIFF_TQ:(c + 1) * DIFF_TQ, :]
        zero = jnp.zeros_like(q)
        dn = (((1,), (1,)), ((), ()))
        return (lax.dot_general(jnp.where(first, q, zero), k, dn, preferred_element_type=F32),
                lax.dot_general(jnp.where(first, zero, q), k, dn, preferred_element_type=F32))

    def softmax_parts(sc):
        p = jnp.exp2(sc - jnp.max(sc, axis=-1, keepdims=True))
        return p.astype(BF16), jnp.sum(p, axis=-1, keepdims=True)

    pending = [scores(c) for c in range(min(DIFF_AHEAD, DIFF_CHAINS))]
    for c in range(DIFF_CHAINS):
        if c + DIFF_AHEAD < DIFF_CHAINS:
            pending.append(scores(c + DIFF_AHEAD))
        sc = pending.pop(0)
        p0, l0 = softmax_parts(sc[0])
        p1, l1 = softmax_parts(sc[1])
        a = p0 * (1.0 / l0).astype(BF16) - p1 * (lam / l1).astype(BF16)
        o = jnp.dot(a, v, preferred_element_type=F32)
        o = _rms_rows(o, sub_ref[...], DIFF_SUBLN_EPS) * (1.0 - lam_init)
        o_ref[0, 0, c * DIFF_TQ:(c + 1) * DIFF_TQ, :] = o.astype(BF16)


def _diff_attention(q, k, v, lam_params, subln, lam_init):
    b, heads, _, s, _ = q.shape
    tq = DIFF_TQ * DIFF_CHAINS
    return pl.pallas_call(
        functools.partial(_diff_attn_kernel, lam_init=lam_init),
        grid=(b, heads, s // tq),
        in_specs=[
            pl.BlockSpec((1, 1, 1, tq, LANES), lambda bi, h, i: (bi, h, 0, i, 0)),
            pl.BlockSpec((1, 1, 1, s, LANES), lambda bi, h, i: (bi, h, 0, 0, 0)),
            pl.BlockSpec((1, 1, 1, s, LANES), lambda bi, h, i: (bi, h, 0, 0, 0)),
            pl.BlockSpec(lam_params.shape, lambda bi, h, i: (0, 0)),
            pl.BlockSpec((1, LANES), lambda bi, h, i: (0, 0)),
        ],
        out_specs=pl.BlockSpec((1, 1, tq, LANES), lambda bi, h, i: (bi, h, i, 0)),
        out_shape=jax.ShapeDtypeStruct((b, heads, s, LANES), BF16),
        compiler_params=_params("parallel", "parallel", "parallel"),
        name="diff_attention",
    )(q, k, v, lam_params, subln.reshape(1, LANES))


LRU_PAD = 8
LRU_ROWS = 256
LRU_PROJ_ROWS = 1024
LRU_CW = 512
LRU_SCAN_TILES = 8


def _log_sigmoid(x):
    return jnp.minimum(x, 0.0) - jnp.log1p(jnp.exp(-jnp.abs(x)))


def _gelu_tanh(x):
    return 0.5 * x * (1.0 + jnp.tanh(math.sqrt(2.0 / math.pi) * (x + 0.044715 * (x * x * x))))


def _lru_kernel(h_ref, g_ref, wg_ref, wu_ref, cw_ref, cb_ref, wbd_ref, bias_ref, lam_ref, o_ref,
                xn_ref, gate_ref, upad_ref, a_ref, b_ref, c_ref):
    s = h_ref.shape[1]
    cw = wg_ref.shape[1]
    n_slab = cw // LANES
    assert s == SUBLANES * LRU_ROWS

    @pl.when(pl.program_id(1) == 0)
    def _():
        def norm_step(i, carry):
            r0 = pl.multiple_of(i * LRU_ROWS, LRU_ROWS)
            xn_ref[pl.ds(r0, LRU_ROWS), :] = _rms_rows(h_ref[0, pl.ds(r0, LRU_ROWS), :], g_ref[...],
                                                       NORM_EPS).astype(BF16)
            return carry

        lax.fori_loop(0, s // LRU_ROWS, norm_step, 0)

    zeros = jnp.zeros((LRU_PAD, cw), F32)
    upad_ref[0:LRU_PAD, :] = zeros
    upad_ref[LRU_PAD + s:LRU_PAD + s + LRU_PAD, :] = zeros
    wgu = jnp.concatenate([wg_ref[...], wu_ref[...]], axis=-1)
    for blk in range(s // LRU_PROJ_ROWS):
        rows = slice(blk * LRU_PROJ_ROWS, (blk + 1) * LRU_PROJ_ROWS)
        gu = jnp.dot(xn_ref[rows, :], wgu, preferred_element_type=F32)
        gate_ref[rows, :] = gu[:, :cw]
        upad_ref[LRU_PAD + blk * LRU_PROJ_ROWS:LRU_PAD + (blk + 1) * LRU_PROJ_ROWS, :] = gu[:, cw:]

    row8 = lax.broadcasted_iota(jnp.int32, (SUBLANES, 1), 0)
    for dr in range(2):
        half_unit = (0.5 * LRU_C) * _log_sigmoid(lam_ref[dr:dr + 1, :])

        def gate_step(i, carry, dr=dr, half_unit=half_unit):
            r0 = pl.multiple_of(i * LRU_ROWS, LRU_ROWS)
            uw = _halo_window(upad_ref, LRU_PAD + r0, LRU_ROWS)
            xc = cb_ref[dr:dr + 1, :]
            for j in range(LRU_CONV):
                off = (j - (LRU_CONV - 1)) if dr == 0 else ((LRU_CONV - 1) - j)
                xc = xc + cw_ref[dr, j:j + 1, :] * _shift_rows(uw, off)[SUBLANES:SUBLANES + LRU_ROWS]
            pre = jnp.dot(xc.astype(BF16), wbd_ref[dr, 0], preferred_element_type=F32) + bias_ref[0, dr:dr + 1, :]
            t_r = jnp.tanh(pre[:, :cw])
            t_i = jnp.tanh(pre[:, cw:])
            log_a = t_r * half_unit + half_unit
            a = jnp.exp(log_a)
            th = jnp.tanh(log_a)
            bv = jnp.sqrt(-0.5 * th / (1.0 - th)) * ((t_i + 1.0) * xc)
            for sl in range(n_slab):
                lanes = slice(sl * LANES, (sl + 1) * LANES)
                a_ref[dr, sl, pl.ds(i, LRU_ROWS, stride=SUBLANES), :] = a[:, lanes]
                b_ref[dr, sl, pl.ds(i, LRU_ROWS, stride=SUBLANES), :] = bv[:, lanes]
            return carry

        lax.fori_loop(0, SUBLANES, gate_step, 0)

        n_blocks = LRU_ROWS // LRU_SCAN_TILES

        def scan_block(tb, carry, dr=dr):
            blk = tb if dr == 0 else n_blocks - 1 - tb
            base = pl.multiple_of(blk * (LRU_SCAN_TILES * SUBLANES), LRU_SCAN_TILES * SUBLANES)
            hs, ps = list(carry[0]), list(carry[1])
            for u in range(LRU_SCAN_TILES):
                tile = u if dr == 0 else LRU_SCAN_TILES - 1 - u
                rows = pl.ds(base + tile * SUBLANES, SUBLANES)
                for sl in range(n_slab):
                    a_t = a_ref[dr, sl, rows, :]
                    hs[sl] = a_t * hs[sl] + b_ref[dr, sl, rows, :]
                    ps[sl] = a_t * ps[sl]
                    b_ref[dr, sl, rows, :] = hs[sl]
                    a_ref[dr, sl, rows, :] = ps[sl]
            return tuple(hs), tuple(ps)

        init = (tuple(jnp.zeros((SUBLANES, LANES), F32) for _ in range(n_slab)),
                tuple(jnp.ones((SUBLANES, LANES), F32) for _ in range(n_slab)))
        h_end, p_end = lax.fori_loop(0, n_blocks, scan_block, init)

        for sl in range(n_slab):
            c = jnp.zeros((SUBLANES, LANES), F32)
            for _ in range(SUBLANES - 1):
                nxt = h_end[sl] + p_end[sl] * c
                if dr == 0:
                    c = jnp.where(row8 == 0, 0.0, _shift_rows(nxt, -1))
                else:
                    c = jnp.where(row8 == SUBLANES - 1, 0.0, _shift_rows(nxt, 1))
            c_ref[dr, sl] = c

    def fix_step(t, carry):
        r0 = pl.multiple_of(t * LRU_ROWS, LRU_ROWS)
        for sl in range(n_slab):
            tot = None
            for dr in range(2):
                local = b_ref[dr, sl, pl.ds(r0, LRU_ROWS), :].reshape(-1, SUBLANES, LANES)
                decay = a_ref[dr, sl, pl.ds(r0, LRU_ROWS), :].reshape(-1, SUBLANES, LANES)
                fixed = local + decay * c_ref[dr, sl]
                tot = fixed if tot is None else tot + fixed
            b_ref[0, sl, pl.ds(r0, LRU_ROWS), :] = tot.reshape(LRU_ROWS, LANES)
        return carry

    lax.fori_loop(0, SUBLANES, fix_step, 0)

    for i in range(SUBLANES):
        rows = slice(i * LRU_ROWS, (i + 1) * LRU_ROWS)
        hsum = jnp.concatenate([b_ref[0, sl, pl.ds(i, LRU_ROWS, stride=SUBLANES), :] for sl in range(n_slab)],
                               axis=-1)
        o_ref[0, rows, :] = (_gelu_tanh(gate_ref[rows, :]) * hsum).astype(BF16)


def _lru_core(h, gain, w_in, conv_w, conv_b, w_a, b_a, w_x, b_x, lam):
    b, s, dm = h.shape
    c = w_in.shape[1] // 2
    cw = LRU_CW
    ncb = c // cw
    per = cw // HEAD_DIM
    eye = jnp.eye(per, dtype=F32)

    def block_diag(w):
        w5 = w.reshape(2, ncb, per, HEAD_DIM, HEAD_DIM)
        return jnp.einsum('dcipq,ik->dcipkq', w5, eye).reshape(2, ncb, cw, cw)

    wbd = (0.5 * jnp.concatenate([block_diag(w_a), block_diag(w_x)], axis=-1)).astype(BF16)
    bias = 0.5 * jnp.concatenate([b_a.reshape(2, ncb, cw), b_x.reshape(2, ncb, cw)], axis=-1)
    bias = bias.transpose(1, 0, 2)
    n_slab = cw // LANES
    return pl.pallas_call(
        _lru_kernel,
        grid=(b, ncb),
        in_specs=[
            pl.BlockSpec((1, s, dm), lambda bi, cb: (bi, 0, 0), pipeline_mode=pl.Buffered(1)),
            pl.BlockSpec((1, dm), lambda bi, cb: (0, 0)),
            pl.BlockSpec((dm, cw), lambda bi, cb: (0, cb)),
            pl.BlockSpec((dm, cw), lambda bi, cb: (0, ncb + cb)),
            pl.BlockSpec((2, LRU_CONV, cw), lambda bi, cb: (0, 0, cb)),
            pl.BlockSpec((2, cw), lambda bi, cb: (0, cb)),
            pl.BlockSpec((2, 1, cw, 2 * cw), lambda bi, cb: (0, cb, 0, 0)),
            pl.BlockSpec((1, 2, 2 * cw), lambda bi, cb: (cb, 0, 0)),
            pl.BlockSpec((2, cw), lambda bi, cb: (0, cb)),
        ],
        out_specs=pl.BlockSpec((1, s, cw), lambda bi, cb: (bi, 0, cb)),
        out_shape=jax.ShapeDtypeStruct((b, s, c), BF16),
        scratch_shapes=[
            pltpu.VMEM((s, dm), BF16),
            pltpu.VMEM((s, cw), F32),
            pltpu.VMEM((s + 2 * LRU_PAD, cw), F32),
            pltpu.VMEM((2, n_slab, s, LANES), F32),
            pltpu.VMEM((2, n_slab, s, LANES), F32),
            pltpu.VMEM((2, n_slab, SUBLANES, LANES), F32),
        ],
        compiler_params=_params("parallel", "arbitrary"),
        name="lru_core",
    )(h, gain.reshape(1, dm), w_in, w_in, conv_w, conv_b, wbd, bias, lam)


DIL_TQ = 128
DIL_UNROLL = 4


def _band_bias(tq, w, rel0, half):
    qi = lax.broadcasted_iota(jnp.int32, (tq, 1), 0)
    ki = lax.broadcasted_iota(jnp.int32, (1, w), 1)
    return jnp.where(jnp.abs(ki + rel0 - qi) <= half, 0.0, NEG_INF).astype(F32)


def _band_scores(q, k):
    lane = lax.broadcasted_iota(jnp.int32, (1, LANES), 1)
    first = lane < HEAD_DIM
    zero = jnp.zeros_like(q)
    qq = jnp.concatenate([jnp.where(first, q, zero), jnp.where(first, zero, q)], axis=0)
    return lax.dot_general(qq, k, (((1,), (1,)), ((), ())), preferred_element_type=F32)


def _band_softmax(sc, bias):
    sc = sc + jnp.concatenate([bias, bias], axis=0)
    m = jnp.max(sc, axis=-1, keepdims=True)
    p = jnp.exp2(sc - m).astype(BF16)
    return p, jnp.broadcast_to(m, (sc.shape[0], LANES))


def _band_pv(p, m_b, v):
    tq = p.shape[0] // 2
    lane = lax.broadcasted_iota(jnp.int32, (1, LANES), 1)
    first = lane < HEAD_DIM
    v1 = jnp.concatenate([v, jnp.ones_like(v)], axis=-1)
    pv = jnp.dot(p, v1, preferred_element_type=F32)
    pick = lambda t: jnp.where(first, t[:tq], t[tq:])
    return pick(m_b), pick(pv[:, LANES:]), pick(pv[:, :LANES])


def _dilated_attn_kernel(q0_ref, k0_ref, v0_ref, q1_ref, k1_ref, v1_ref, q2_ref, k2_ref, v2_ref,
                         o_ref, m_ref, l_ref, acc_ref, bias_ref, p_ref, mb_ref, *, halves, dils):
    s = o_ref.shape[2]
    tq = DIL_TQ
    win = 2 * tq

    def merge(rows, m_new, l_new, acc_new):
        m_old, l_old, acc_old = m_ref[rows, :], l_ref[rows, :], acc_ref[rows, :]
        m_tot = jnp.maximum(m_old, m_new)
        alpha = jnp.exp2(m_old - m_tot)
        beta = jnp.exp2(m_new - m_tot)
        m_ref[rows, :] = m_tot
        l_ref[rows, :] = alpha * l_old + beta * l_new
        acc_ref[rows, :] = alpha * acc_old + beta * acc_new

    groups = ((q0_ref, k0_ref, v0_ref), (q1_ref, k1_ref, v1_ref), (q2_ref, k2_ref, v2_ref))
    order = sorted(range(len(groups)), key=lambda gi: -dils[gi])
    for g in order:
        q_ref, k_ref, v_ref = groups[g]
        half, d = halves[g], dils[g]
        n = s // d
        tiles = n // tq
        w = min(win, n)
        for case, rel0 in enumerate((0, -(tq // 2), tq - w) if tiles > 1 else (0,)):
            bias_ref[case, :, 0:w] = _band_bias(tq, w, rel0, half)

        def tile_coords(c, n=n, tiles=tiles, w=w):
            r = c // tiles
            i = c % tiles
            q0 = pl.multiple_of(i * tq, tq)
            ks = pl.multiple_of(jnp.clip(q0 - tq // 2, 0, n - w), tq // 2)
            case = jnp.where(i == 0, 0, jnp.where(i == tiles - 1, 2, 1)) if tiles > 1 else 0
            return r, q0, ks, case

        def score_stage(cb, slot, q_ref=q_ref, k_ref=k_ref, w=w):
            for u in range(DIL_UNROLL):
                r, q0, ks, case = tile_coords(cb * DIL_UNROLL + u)
                sc = _band_scores(q_ref[0, 0, r, pl.ds(q0, tq), :], k_ref[0, 0, r, pl.ds(ks, w), :])
                p, m_b = _band_softmax(sc, bias_ref[case, :, 0:w])
                p_ref[slot, u, :, 0:w] = p
                mb_ref[slot, u] = m_b

        def value_stage(cb, slot, g=g, v_ref=v_ref, d=d, w=w):
            for u in range(DIL_UNROLL):
                r, q0, ks, _ = tile_coords(cb * DIL_UNROLL + u)
                m_new, l_new, acc_new = _band_pv(p_ref[slot, u, :, 0:w], mb_ref[slot, u],
                                                 v_ref[0, 0, r, pl.ds(ks, w), :])
                rows = pl.ds(q0, tq) if d == 1 else pl.ds(q0 * d + r, tq, stride=d)
                if g == order[0]:
                    m_ref[rows, :] = m_new
                    l_ref[rows, :] = l_new
                    acc_ref[rows, :] = acc_new
                else:
                    merge(rows, m_new, l_new, acc_new)

        assert (d * tiles) % DIL_UNROLL == 0
        n_batches = d * tiles // DIL_UNROLL
        score_stage(0, 0)

        def pipe_step(cb, carry, score_stage=score_stage, value_stage=value_stage):
            value_stage(cb - 1, (cb - 1) % 2)
            score_stage(cb, cb % 2)
            return carry

        lax.fori_loop(1, n_batches, pipe_step, 0)
        value_stage(n_batches - 1, (n_batches - 1) % 2)

    def out_step(i, carry):
        r0 = pl.multiple_of(i * 256, 256)
        o_ref[0, 0, pl.ds(r0, 256), :] = (acc_ref[pl.ds(r0, 256), :] / l_ref[pl.ds(r0, 256), :]).astype(BF16)
        return carry

    lax.fori_loop(0, s // 256, out_step, 0)


def _dilated_attention(qkv_groups):
    b, pairs, d0, s, _ = qkv_groups[0][0].shape
    assert d0 == 1
    halves = tuple(w // (2 * d) for (w, d) in DIL_GROUPS)
    dils = tuple(d for (_, d) in DIL_GROUPS)
    in_specs, args = [], []
    for q, k, v in qkv_groups:
        d, n = q.shape[2], q.shape[3]
        for arr in (q, k, v):
            in_specs.append(pl.BlockSpec((1, 1, d, n, LANES), lambda bi, p: (bi, p, 0, 0, 0)))
            args.append(arr)
    return pl.pallas_call(
        functools.partial(_dilated_attn_kernel, halves=halves, dils=dils),
        grid=(b, pairs),
        in_specs=in_specs,
        out_specs=pl.BlockSpec((1, 1, s, LANES), lambda bi, p: (bi, p, 0, 0)),
        out_shape=jax.ShapeDtypeStruct((b, pairs, s, LANES), BF16),
        scratch_shapes=[pltpu.VMEM((s, LANES), F32), pltpu.VMEM((s, LANES), F32), pltpu.VMEM((s, LANES), F32),
                        pltpu.VMEM((3, DIL_TQ, 2 * DIL_TQ), F32),
                        pltpu.VMEM((2, DIL_UNROLL, 2 * DIL_TQ, 2 * DIL_TQ), BF16),
                        pltpu.VMEM((2, DIL_UNROLL, 2 * DIL_TQ, LANES), F32)],
        compiler_params=_params("parallel", "parallel"),
        name="dilated_attention",
    )(*args)


FFN_HALO = 16
FFN_UP_ROWS = 1024
FFN_DOWN_ROWS = 512
FFN_CHUNK = 256
FFN_STEP_CHUNKS = 4


def _ffn_kernel(h_ref, g_ref, wg_ref, *rest, final_norm, n_chunks):
    wu_refs = rest[:FFN_STEP_CHUNKS]
    cw_ref, cb_ref, wd_ref, hr_ref, fg_ref, o_ref, xn_ref, hid_ref = rest[FFN_STEP_CHUNKS:]
    j = pl.program_id(1)
    s = h_ref.shape[1]
    n_full = n_chunks // FFN_STEP_CHUNKS
    n_up = -(-n_chunks // FFN_STEP_CHUNKS)

    @pl.when(j == 0)
    def _():
        zeros = jnp.zeros((FFN_HALO, xn_ref.shape[1]), BF16)
        xn_ref[0:FFN_HALO, :] = zeros
        xn_ref[FFN_HALO + s:FFN_HALO + s + FFN_HALO, :] = zeros

        def norm_step(i, carry):
            r0 = pl.multiple_of(i * FFN_DOWN_ROWS, FFN_DOWN_ROWS)
            x = h_ref[0, pl.ds(r0, FFN_DOWN_ROWS), :]
            xn_ref[pl.ds(FFN_HALO + r0, FFN_DOWN_ROWS), :] = _rms_rows(x, g_ref[...], NORM_EPS).astype(BF16)
            return carry

        lax.fori_loop(0, s // FFN_DOWN_ROWS, norm_step, 0)

    def up_chunks(n_here):
        inner = slice(FFN_HALO, FFN_HALO + FFN_UP_ROWS)
        jobs = []
        for k in range(n_here):
            c = j * FFN_STEP_CHUNKS + k
            cols = slice(k * FFN_CHUNK, (k + 1) * FFN_CHUNK)
            wgu = jnp.concatenate([wg_ref[:, cols], wu_refs[k][...]], axis=-1)
            jobs += [(c, cols, wgu, rc) for rc in range(s // FFN_UP_ROWS)]

        def matmul(job):
            _, _, wgu, rc = job
            xs = xn_ref[rc * FFN_UP_ROWS:rc * FFN_UP_ROWS + FFN_UP_ROWS + 2 * FFN_HALO, :]
            return jnp.dot(xs, wgu, preferred_element_type=F32)

        def epilogue(job, gu):
            c, cols, _, rc = job
            g = gu[:, :FFN_CHUNK]
            gc = cb_ref[:, cols]
            for t in range(FFN_CONV):
                gc = gc + cw_ref[t:t + 1, cols] * _shift_rows(g, t - FFN_CONV // 2)[inner]
            act = 0.5 * gc * (1.0 + lax.erf(gc * (1.0 / math.sqrt(2.0))))
            hid_ref[c, rc * FFN_UP_ROWS:(rc + 1) * FFN_UP_ROWS, :] = (act * gu[inner, FFN_CHUNK:]).astype(BF16)

        pending = matmul(jobs[0])
        for n, job in enumerate(jobs):
            nxt = matmul(jobs[n + 1]) if n + 1 < len(jobs) else None
            epilogue(job, pending)
            pending = nxt

    @pl.when(j < n_full)
    def _():
        up_chunks(FFN_STEP_CHUNKS)

    if n_chunks % FFN_STEP_CHUNKS:
        @pl.when(j == n_full)
        def _():
            up_chunks(n_chunks % FFN_STEP_CHUNKS)

    @pl.when(j >= n_up)
    def _():
        r0 = pl.multiple_of((j - n_up) * FFN_DOWN_ROWS, FFN_DOWN_ROWS)
        hid = jnp.concatenate([hid_ref[c, pl.ds(r0, FFN_DOWN_ROWS), :] for c in range(n_chunks)], axis=-1)
        out = hr_ref[0] + jnp.dot(hid, wd_ref[...], preferred_element_type=F32)
        if final_norm:
            out = _rms_rows(out, fg_ref[...], NORM_EPS)
        o_ref[0] = out


def _conv_ffn(h, gain, w_up, conv_w, conv_b, w_down, final_gain):
    b, s, dm = h.shape
    dff = w_down.shape[0]
    n_chunks = dff // FFN_CHUNK
    per = FFN_STEP_CHUNKS
    n_up = -(-n_chunks // per)
    n_down = s // FFN_DOWN_ROWS
    assert dff % FFN_CHUNK == 0 and s % FFN_UP_ROWS == 0 and s % FFN_DOWN_ROWS == 0
    final_norm = final_gain is not None
    fg = (final_gain if final_norm else gain).reshape(1, dm)
    pad = per * n_up * FFN_CHUNK - dff
    conv_w = jnp.pad(conv_w, ((0, 0), (0, pad)))
    conv_b = jnp.pad(conv_b.reshape(1, dff), ((0, 0), (0, pad)))
    up_idx = lambda j: jnp.minimum(j, n_up - 1)
    down_idx = lambda j: jnp.maximum(j - n_up, 0)
    value_specs = [pl.BlockSpec((dm, FFN_CHUNK), lambda bi, j, k=k: (0, jnp.minimum(n_chunks + per * up_idx(j) + k,
                                                                                   2 * n_chunks - 1)))
                   for k in range(per)]
    return pl.pallas_call(
        functools.partial(_ffn_kernel, final_norm=final_norm, n_chunks=n_chunks),
        grid=(b, n_up + n_down),
        in_specs=[
            pl.BlockSpec((1, s, dm), lambda bi, j: (bi, 0, 0)),
            pl.BlockSpec((1, dm), lambda bi, j: (0, 0)),
            pl.BlockSpec((dm, per * FFN_CHUNK), lambda bi, j: (0, up_idx(j))),
            *value_specs,
            pl.BlockSpec((FFN_CONV, per * FFN_CHUNK), lambda bi, j: (0, up_idx(j))),
            pl.BlockSpec((1, per * FFN_CHUNK), lambda bi, j: (0, up_idx(j))),
            pl.BlockSpec((dff, dm), lambda bi, j: (0, 0), pipeline_mode=pl.Buffered(1)),
            pl.BlockSpec((1, FFN_DOWN_ROWS, dm), lambda bi, j: (bi, down_idx(j), 0)),
            pl.BlockSpec((1, dm), lambda bi, j: (0, 0)),
        ],
        out_specs=pl.BlockSpec((1, FFN_DOWN_ROWS, dm), lambda bi, j: (bi, down_idx(j), 0)),
        out_shape=jax.ShapeDtypeStruct(h.shape, F32),
        scratch_shapes=[pltpu.VMEM((s + 2 * FFN_HALO, dm), BF16), pltpu.VMEM((n_chunks, s, FFN_CHUNK), BF16)],
        compiler_params=_params("parallel", "arbitrary"),
        name="conv_ffn",
    )(h, gain.reshape(1, dm), w_up, *([w_up] * per), conv_w, conv_b, w_down, h, fg)


def kernel(x, positions, mix_norm, pool_w, pool_scale, diff_w_qkv, diff_lam_q1, diff_lam_k1, diff_lam_q2,
           diff_lam_k2, diff_subln, diff_w_o, lru_w_in, lru_conv_w, lru_conv_b, lru_w_a, lru_b_a, lru_w_x,
           lru_b_x, lru_lambda, lru_w_out, dil_w_qkv, dil_w_o, ffn_norm, ffn_w_up, ffn_conv_w, ffn_conv_b,
           ffn_w_down, final_norm):
    b, s, dm = x.shape
    depth = mix_norm.shape[0]
    cos, sin = _rope_tables(positions)
    h = x
    for i in range(depth):
        m, j = i % N_MIXERS, i // N_MIXERS
        if m == 0:
            h = _pool_mixer(h, mix_norm[i], pool_w[j], pool_scale[j])
        elif m == 1:
            q, k, v = _qkv_proj(h, mix_norm[i], diff_w_qkv[j].astype(BF16), 0, cos, sin, 1)
            lam_params = jnp.zeros((SUBLANES, LANES), F32)
            lam_params = lam_params.at[0:4, 0:HEAD_DIM].set(
                jnp.stack([diff_lam_q1[j], diff_lam_k1[j], diff_lam_q2[j], diff_lam_k2[j]]).astype(F32))
            lam_init = 0.8 - 0.6 * math.exp(-0.3 * i)
            o = _diff_attention(q, k, v, lam_params, diff_subln[j], lam_init)
            h = _out_proj_hm(o, diff_w_o[j].astype(BF16), h)
        elif m == 2:
            y = _lru_core(h, mix_norm[i], lru_w_in[j].astype(BF16), lru_conv_w[j], lru_conv_b[j], lru_w_a[j],
                          lru_b_a[j], lru_w_x[j], lru_b_x[j], lru_lambda[j])
            h = _out_proj(y, lru_w_out[j].astype(BF16), h)
        else:
            w = dil_w_qkv[j].astype(BF16)
            groups = [_qkv_proj(h, mix_norm[i], w, g * 3 * dm, cos, sin, d)
                      for g, (_, d) in enumerate(DIL_GROUPS)]
            o = _dilated_attention(groups)
            h = _out_proj_hm(o, dil_w_o[j].astype(BF16), h)
        h = _conv_ffn(h, ffn_norm[i], ffn_w_up[i].astype(BF16), ffn_conv_w[i], ffn_conv_b[i],
                      ffn_w_down[i].astype(BF16), final_norm if i == depth - 1 else None)
    return h
```
